```python
import math
import jax, jax.numpy as jnp
from jax import lax
import numpy as np

D_MODEL = 2048
BATCH = 4
SEQ = 2048
DEPTH = 4
DEC_BATCH = 128
DEC_SEQ = 8
PAST_LEN = 8192
PAGE_SIZE = 128

N_MIXERS = 3
LAYER_MIXER = tuple(i % N_MIXERS for i in range(DEPTH))
N_A = sum(1 for m in LAYER_MIXER if m == 0)
N_B = sum(1 for m in LAYER_MIXER if m == 1)
N_C = sum(1 for m in LAYER_MIXER if m == 2)
EPS = 1e-6

GLA_HEADS = 4
GLA_DK = D_MODEL // 2 // GLA_HEADS
GLA_DV = D_MODEL // GLA_HEADS
GLA_GATE_RANK = 16
GLA_TAU = 16.0
GLA_CHUNK = 64
GLA_IN = 2 * GLA_HEADS * GLA_DK + 2 * GLA_HEADS * GLA_DV + GLA_GATE_RANK

MLA_HEADS = D_MODEL // 128
MLA_NOPE = 128
MLA_ROPE = 64
MLA_VH = 128
MLA_KV_RANK = 512
ROPE_BASE = 10000.0
Q_BLOCK = 128
MLA_SCALE = (MLA_NOPE + MLA_ROPE) ** -0.5
MLA_IN = MLA_HEADS * (MLA_NOPE + MLA_ROPE) + MLA_KV_RANK + MLA_ROPE + MLA_HEADS * MLA_VH

POOL_WINDOWS = (2, 4, 8, 16)
POOL_GROUPS = len(POOL_WINDOWS)
POOL_W = D_MODEL
POOL_G = POOL_W // POOL_GROUPS
POOL_HIST = max(POOL_WINDOWS) - 1

kernel_name = 'hybrid_gla_mla_pool_adaln_step'


def rmsnorm(x, g):
    x32 = x.astype(jnp.float32)
    y = x32 * lax.rsqrt(jnp.mean(x32 * x32, axis=-1, keepdims=True) + EPS)
    return (y * g.astype(jnp.float32)).astype(x.dtype)


def rope_angles(pos):
    inv = ROPE_BASE ** (-jnp.arange(0, MLA_ROPE, 2, dtype=jnp.float32) / MLA_ROPE)
    ang = pos.astype(jnp.float32)[:, None] * inv[None, :]
    return jnp.cos(ang), jnp.sin(ang)


def apply_rope(x, cos, sin):
    x32 = x.astype(jnp.float32)
    x1, x2 = jnp.split(x32, 2, axis=-1)
    return jnp.concatenate([x1 * cos - x2 * sin, x2 * cos + x1 * sin], axis=-1).astype(x.dtype)


def gla_chunk_scan(q, k, v, gk, s0):
    b, t, h, _ = q.shape
    dv = v.shape[-1]
    c = GLA_CHUNK if t % GLA_CHUNK == 0 else t
    n = t // c

    def to_chunks(a):
        return a.reshape(b, n, c, h, a.shape[-1]).transpose(1, 0, 3, 2, 4)

    causal = jnp.tril(jnp.ones((c, c), dtype=bool))

    def step(s, inp):
        qc, kc, vc, gc = inp
        cum = jnp.cumsum(gc, axis=2)
        inter = jnp.einsum('bhtd,bhde->bhte', qc * jnp.exp(cum), s)
        diff = cum[:, :, :, None, :] - cum[:, :, None, :, :]
        decay = jnp.exp(jnp.where(causal[None, None, :, :, None], diff, -jnp.inf))
        att = jnp.einsum('bhtd,bhsd,bhtsd->bhts', qc, kc, decay)
        o = inter + jnp.einsum('bhts,bhse->bhte', att, vc)
        last = cum[:, :, -1:, :]
        s_new = jnp.exp(last[:, :, 0, :])[..., None] * s + jnp.einsum('bhsd,bhse->bhde', kc * jnp.exp(last - cum), vc)
        return s_new, o

    s_fin, o = lax.scan(step, s0, (to_chunks(q), to_chunks(k), to_chunks(v), to_chunks(gk)))
    o = o.transpose(1, 0, 3, 2, 4).reshape(b, t, h, dv)
    return o, s_fin


def gla_mixer(h, s0, w_in, w_gate_up, b_gate, onorm_g, w_out):
    b, t, _ = h.shape
    hk, hv = GLA_HEADS * GLA_DK, GLA_HEADS * GLA_DV
    f32 = jnp.float32
    q, k, v, glr, og = jnp.split(h @ w_in, [hk, 2 * hk, 2 * hk + hv, 2 * hk + hv + GLA_GATE_RANK], axis=-1)
    q = q.reshape(b, t, GLA_HEADS, GLA_DK).astype(f32) * (GLA_DK ** -0.5)
    k = k.reshape(b, t, GLA_HEADS, GLA_DK).astype(f32)
    v = v.reshape(b, t, GLA_HEADS, GLA_DV).astype(f32)
    gk = jax.nn.log_sigmoid((glr @ w_gate_up + b_gate).astype(f32)) / GLA_TAU
    gk = gk.reshape(b, t, GLA_HEADS, GLA_DK)
    o, s_new = gla_chunk_scan(q, k, v, gk, s0.astype(f32))
    o = rmsnorm(o, onorm_g).reshape(b, t, hv).astype(h.dtype)
    y = (o * jax.nn.silu(og)) @ w_out
    return y, s_new.astype(s0.dtype)


def mla_project(h, pos, w_in, kv_norm_g):
    b, t, _ = h.shape
    nq = MLA_HEADS * (MLA_NOPE + MLA_ROPE)
    q, ckv, kr, gate = jnp.split(h @ w_in, [nq, nq + MLA_KV_RANK, nq + MLA_KV_RANK + MLA_ROPE], axis=-1)
    q = q.reshape(b, t, MLA_HEADS, MLA_NOPE + MLA_ROPE)
    q_nope, q_rope = q[..., :MLA_NOPE], q[..., MLA_NOPE:]
    cos, sin = rope_angles(pos)
    q_rope = apply_rope(q_rope, cos[:, None, :], sin[:, None, :])
    kr = apply_rope(kr, cos, sin)
    ckv = rmsnorm(ckv, kv_norm_g)
    return q_nope, q_rope, ckv, kr, gate


def mla_prompt_attend(q_nope, q_rope, ckv, kr, w_uk, w_uv):
    b, t, h, _ = q_nope.shape
    k_nope = jnp.einsum('bsc,chd->bshd', ckv, w_uk)
    v = jnp.einsum('bsc,chd->bshd', ckv, w_uv)
    nb = t // Q_BLOCK
    kpos = jnp.arange(t)

    def blocks(a):
        return a.reshape(b, nb, Q_BLOCK, *a.shape[2:]).swapaxes(0, 1)

    def one_block(inp):
        qn, qr, i = inp
        s = (jnp.einsum('bqhd,bshd->bhqs', qn, k_nope, preferred_element_type=jnp.float32)
             + jnp.einsum('bqhr,bsr->bhqs', qr, kr, preferred_element_type=jnp.float32))
        qpos = i * Q_BLOCK + jnp.arange(Q_BLOCK)
        s = jnp.where(kpos[None, :] <= qpos[:, None], s * MLA_SCALE, -jnp.inf)
        p = jax.nn.softmax(s, axis=-1).astype(v.dtype)
        return jnp.einsum('bhqs,bshd->bqhd', p, v)

    o = lax.map(one_block, (blocks(q_nope), blocks(q_rope), jnp.arange(nb)))
    return o.swapaxes(0, 1).reshape(b, t, h, MLA_VH)


def mla_sample_attend(q_nope, q_rope, ckv, kr, cache_ckv, cache_kr, layer, page_table, w_uk, w_uv):
    b, t, h, _ = q_nope.shape
    past = page_table.shape[1] * PAGE_SIZE
    ckv_all = jnp.concatenate([cache_ckv[layer, page_table].reshape(b, past, MLA_KV_RANK).astype(ckv.dtype), ckv], axis=1)
    kr_all = jnp.concatenate([cache_kr[layer, page_table].reshape(b, past, MLA_ROPE).astype(kr.dtype), kr], axis=1)
    q_lat = jnp.einsum('bthd,chd->bthc', q_nope, w_uk)
    s = (jnp.einsum('bthc,bsc->bhts', q_lat, ckv_all, preferred_element_type=jnp.float32)
         + jnp.einsum('bthr,bsr->bhts', q_rope, kr_all, preferred_element_type=jnp.float32))
    kpos = jnp.arange(past + t)
    qpos = past + jnp.arange(t)
    s = jnp.where(kpos[None, :] <= qpos[:, None], s * MLA_SCALE, -jnp.inf)
    p = jax.nn.softmax(s, axis=-1).astype(ckv_all.dtype)
    o_lat = jnp.einsum('bhts,bsc->bthc', p, ckv_all)
    return jnp.einsum('bthc,chd->bthd', o_lat, w_uv)


def pool_mixer(h, hist, w_in, w_grp, scale, w_out):
    b, t, _ = h.shape
    f32 = jnp.float32
    u, gate = jnp.split(h @ w_in, 2, axis=-1)
    seq = jnp.concatenate([hist.astype(u.dtype), u], axis=1)
    n_hist = hist.shape[1]
    cs = jnp.concatenate([jnp.zeros((b, 1, POOL_W), f32), jnp.cumsum(seq.astype(f32), axis=1)], axis=1)
    rows = n_hist + jnp.arange(t)
    cs_hi = cs[:, rows + 1]
    u32 = u.astype(f32)
    parts = []
    for g, w in enumerate(POOL_WINDOWS):
        sl = slice(g * POOL_G, (g + 1) * POOL_G)
        lo = jnp.maximum(rows + 1 - w, 0)
        cnt = (rows + 1 - lo).astype(f32)
        mean = (cs_hi[..., sl] - cs[:, lo][..., sl]) / cnt[None, :, None]
        parts.append(mean - u32[..., sl])
    p = jnp.stack(parts, axis=2).astype(h.dtype)
    z = jnp.einsum('btgc,gcd->btgd', p, w_grp).reshape(b, t, POOL_W) * scale
    y = (z * jax.nn.silu(gate)) @ w_out
    return y, seq[:, -POOL_HIST:, :]


def run_group(x, c, pos0, gla_init, pool_init, mla_past, W):
    b, t, _ = x.shape
    pos = pos0 + jnp.arange(t)
    gla_out, ckv_out, kr_out, pool_out = [], [], [], []
    ia = ib = ic = 0
    for l in range(DEPTH):
        mod = jnp.einsum('bd,de->be', jax.nn.silu(c), W['ada_w'][l]) + W['ada_b'][l]
        shift, scale, gate = jnp.split(mod, 3, axis=-1)
        h = rmsnorm(x, W['norm_g'][l]) * (1.0 + scale[:, None, :]) + shift[:, None, :]
        mixer = LAYER_MIXER[l]
        if mixer == 0:
            y, s_new = gla_mixer(h, gla_init[ia], W['gla_w_in'][ia], W['gla_w_gate_up'][ia],
                                 W['gla_b_gate'][ia], W['gla_onorm_g'][ia], W['gla_w_out'][ia])
            gla_out.append(s_new)
            ia += 1
        elif mixer == 1:
            q_nope, q_rope, ckv, kr, g = mla_project(h, pos, W['mla_w_in'][ib], W['mla_kv_norm_g'][ib])
            if mla_past is None:
                o = mla_prompt_attend(q_nope, q_rope, ckv, kr, W['mla_w_uk'][ib], W['mla_w_uv'][ib])
            else:
                cache_ckv, cache_kr, page_table = mla_past
                o = mla_sample_attend(q_nope, q_rope, ckv, kr, cache_ckv, cache_kr, ib, page_table,
                                      W['mla_w_uk'][ib], W['mla_w_uv'][ib])
            y = (o.reshape(b, t, MLA_HEADS * MLA_VH) * jax.nn.silu(g)) @ W['mla_w_out'][ib]
            ckv_out.append(ckv)
            kr_out.append(kr)
            ib += 1
        else:
            y, buf = pool_mixer(h, pool_init[ic], W['pool_w_in'][ic], W['pool_w_grp'][ic],
                                W['pool_scale'][ic], W['pool_w_out'][ic])
            pool_out.append(buf)
            ic += 1
        x = x + gate[:, None, :] * y
    y_out = rmsnorm(x, W['final_norm_g'])
    return y_out, jnp.stack(gla_out), jnp.stack(ckv_out), jnp.stack(kr_out), jnp.stack(pool_out)


def setup_inputs(seed: int = 0) -> dict:
    key = jax.random.key(seed)
    ks = iter(jax.random.split(key, 40))
    f32 = jnp.float32

    def nrm(shape, s):
        return jax.random.normal(next(ks), shape, f32) * s

    D = D_MODEL
    n_pages = PAST_LEN // PAGE_SIZE
    n_used = DEC_BATCH * n_pages
    n_phys = n_used + max(1, n_used // 4)
    page_table = jax.random.permutation(next(ks), n_phys)[:n_used].reshape(DEC_BATCH, n_pages).astype(jnp.int32)
    hk, hv = GLA_HEADS * GLA_DK, GLA_HEADS * GLA_DV
    return {
        'x_prompt': nrm((BATCH, SEQ, D), 1.0),
        'x_sample': nrm((DEC_BATCH, DEC_SEQ, D), 1.0),
        'c_prompt': nrm((BATCH, D), 1.0),
        'c_sample': nrm((DEC_BATCH, D), 1.0),
        'state_gla': nrm((N_A, DEC_BATCH, GLA_HEADS, GLA_DK, GLA_DV), 0.5),
        'cache_ckv': nrm((N_B, n_phys, PAGE_SIZE, MLA_KV_RANK), 1.0),
        'cache_kr': nrm((N_B, n_phys, PAGE_SIZE, MLA_ROPE), 1.0),
        'state_pool': nrm((N_C, DEC_BATCH, POOL_HIST, POOL_W), 1.0),
        'page_table': page_table,
        'norm_g': 1.0 + nrm((DEPTH, D), 0.02),
        'ada_w': nrm((DEPTH, D, 3 * D), 0.5 * D ** -0.5),
        'ada_b': nrm((DEPTH, 3 * D), 0.02),
        'final_norm_g': 1.0 + nrm((D,), 0.02),
        'gla_w_in': nrm((N_A, D, GLA_IN), D ** -0.5),
        'gla_w_gate_up': nrm((N_A, GLA_GATE_RANK, hk), GLA_GATE_RANK ** -0.5),
        'gla_b_gate': nrm((N_A, hk), 0.02),
        'gla_onorm_g': 1.0 + nrm((N_A, GLA_DV), 0.02),
        'gla_w_out': nrm((N_A, hv, D), hv ** -0.5),
        'mla_w_in': nrm((N_B, D, MLA_IN), D ** -0.5),
        'mla_kv_norm_g': 1.0 + nrm((N_B, MLA_KV_RANK), 0.02),
        'mla_w_uk': nrm((N_B, MLA_KV_RANK, MLA_HEADS, MLA_NOPE), MLA_KV_RANK ** -0.5),
        'mla_w_uv': nrm((N_B, MLA_KV_RANK, MLA_HEADS, MLA_VH), MLA_KV_RANK ** -0.5),
        'mla_w_out': nrm((N_B, MLA_HEADS * MLA_VH, D), (MLA_HEADS * MLA_VH) ** -0.5),
        'pool_w_in': nrm((N_C, D, 2 * POOL_W), D ** -0.5),
        'pool_w_grp': nrm((N_C, POOL_GROUPS, POOL_G, POOL_G), POOL_G ** -0.5),
        'pool_scale': 1.0 + nrm((N_C, POOL_W), 0.1),
        'pool_w_out': nrm((N_C, POOL_W, D), POOL_W ** -0.5),
    }


def reference(x_prompt, x_sample, c_prompt, c_sample, state_gla, cache_ckv, cache_kr, state_pool, page_table,
              norm_g, ada_w, ada_b, final_norm_g,
              gla_w_in, gla_w_gate_up, gla_b_gate, gla_onorm_g, gla_w_out,
              mla_w_in, mla_kv_norm_g, mla_w_uk, mla_w_uv, mla_w_out,
              pool_w_in, pool_w_grp, pool_scale, pool_w_out):
    W = {
        'norm_g': norm_g, 'ada_w': ada_w, 'ada_b': ada_b, 'final_norm_g': final_norm_g,
        'gla_w_in': gla_w_in, 'gla_w_gate_up': gla_w_gate_up, 'gla_b_gate': gla_b_gate,
        'gla_onorm_g': gla_onorm_g, 'gla_w_out': gla_w_out,
        'mla_w_in': mla_w_in, 'mla_kv_norm_g': mla_kv_norm_g, 'mla_w_uk': mla_w_uk,
        'mla_w_uv': mla_w_uv, 'mla_w_out': mla_w_out,
        'pool_w_in': pool_w_in, 'pool_w_grp': pool_w_grp, 'pool_scale': pool_scale, 'pool_w_out': pool_w_out,
    }
    b_p = x_prompt.shape[0]
    gla_zero = jnp.zeros((N_A, b_p, GLA_HEADS, GLA_DK, GLA_DV), x_prompt.dtype)
    pool_zero = jnp.zeros((N_C, b_p, 0, POOL_W), x_prompt.dtype)
    y_prompt, gla_p, ckv_p, kr_p, pool_p = run_group(x_prompt, c_prompt, 0, gla_zero, pool_zero, None, W)
    past_len = page_table.shape[1] * PAGE_SIZE
    y_sample, gla_s, ckv_s, kr_s, pool_s = run_group(x_sample, c_sample, past_len, state_gla, state_pool,
                                                     (cache_ckv, cache_kr, page_table), W)
    return (y_prompt, y_sample, gla_p, gla_s, ckv_p, kr_p, ckv_s, kr_s, pool_p, pool_s)
```

```python
import functools

import numpy as np
import jax
import jax.numpy as jnp
from jax import lax
from jax.experimental import pallas as pl
from jax.experimental.pallas import tpu as pltpu

F32 = jnp.float32
BF16 = jnp.bfloat16

D = 2048
BATCH = 4
SEQ = 2048
DEC_BATCH = 128
DEC_SEQ = 8
PAGE = 128
N_PAGES = 64
PAST = N_PAGES * PAGE
DEPTH = 4
LAYER_MIXER = (0, 1, 2, 0)
EPS = 1e-6

NP_TOK = BATCH * SEQ
NS_TOK = DEC_BATCH * DEC_SEQ
N_TOK = NP_TOK + NS_TOK

GLA_H = 4
GLA_DK = 256
GLA_DV = 512
GLA_RANK = 16
GLA_TAU = 16.0
GLA_HK = GLA_H * GLA_DK
GLA_HV = GLA_H * GLA_DV
GLA_C = 128
GLA_SUB = 16

MLA_H = 16
MLA_NOPE = 128
MLA_ROPE = 64
MLA_VH = 128
MLA_RANK = 512
MLA_SCALE = (MLA_NOPE + MLA_ROPE) ** -0.5
MLA_QK = MLA_NOPE + MLA_ROPE
MLA_LAT = MLA_RANK + MLA_ROPE
MLA_N = 5760
ROPE_BASE = 10000.0

POOL_WINDOWS = (2, 4, 8, 16)
POOL_G = 512
POOL_HIST = 15

TM = 512
TM_OUT = 256
VMEM_LIMIT = 56 * 1024 * 1024

_NEG = -1e30


def _params(sem):
    return pltpu.CompilerParams(dimension_semantics=sem, vmem_limit_bytes=VMEM_LIMIT)


def _silu(x):
    return x * (1.0 / (1.0 + jnp.exp(-x)))


def _dot(a, b):
    return jnp.dot(a, b, preferred_element_type=F32)


def _dot_nt(a, b):
    return lax.dot_general(a, b, (((1,), (1,)), ((), ())), preferred_element_type=F32)


def _ada_kernel(c_ref, w_ref, b_ref, o_ref):
    sc = _silu(c_ref[...]).astype(BF16)
    o_ref[0] = _dot(sc, w_ref[0].astype(BF16)) + b_ref[0]


def ada_mod(c_all, ada_w, ada_b):
    rows = c_all.shape[0]
    tn = 768
    return pl.pallas_call(
        _ada_kernel,
        grid=(DEPTH, 3 * D // tn),
        in_specs=[
            pl.BlockSpec((rows, D), lambda l, j: (0, 0)),
            pl.BlockSpec((1, D, tn), lambda l, j: (l, 0, j)),
            pl.BlockSpec((1, 1, tn), lambda l, j: (l, 0, j)),
        ],
        out_specs=pl.BlockSpec((1, rows, tn), lambda l, j: (l, 0, j)),
        out_shape=jax.ShapeDtypeStruct((DEPTH, rows, 3 * D), F32),
        compiler_params=_params(("arbitrary", "arbitrary")),
        name="ada_mod",
    )(c_all, ada_w, ada_b.reshape(DEPTH, 1, 3 * D))


def _mod_rows(i, np_tiles, p_ref, s_ref):
    return jnp.where(i < np_tiles, p_ref[0], s_ref[...])


def _proj_kernel(x_ref, g_ref, shp_ref, scp_ref, shs_ref, scs_ref, w_ref, *rest, np_tiles, with_gk):
    if with_gk:
        wg_ref, wu_ref, bg_ref, o_ref, gk_ref, h_sc = rest
    else:
        o_ref, h_sc = rest
    i = pl.program_id(0)
    j = pl.program_id(1)

    @pl.when(j == 0)
    def _():
        x = x_ref[...]
        ms = jnp.mean(x * x, axis=-1, keepdims=True)
        y = x * lax.rsqrt(ms + EPS) * g_ref[...]
        shift = _mod_rows(i, np_tiles, shp_ref, shs_ref)
        scale = _mod_rows(i, np_tiles, scp_ref, scs_ref)
        h = (y * (1.0 + scale) + shift).astype(BF16)
        h_sc[...] = h
        if with_gk:
            glr = _dot(h, wg_ref[...]).astype(BF16)
            z = _dot(glr, wu_ref[...]) + bg_ref[...]
            gk_ref[...] = (jnp.minimum(z, 0.0) - jnp.log1p(jnp.exp(-jnp.abs(z)))) / GLA_TAU

    o_ref[...] = _dot(h_sc[...], w_ref[...])


def norm_proj(x_all, norm_g, mod_p, mod_s, w, tn, gk_weights=None):
    n = w.shape[1]
    np_tiles = NP_TOK // TM
    tiles_per_seq = SEQ // TM
    grid = (N_TOK // TM, n // tn)

    def p_idx(col):
        return lambda i, j: (jnp.minimum(i // tiles_per_seq, BATCH - 1), 0, col)

    def s_idx(col):
        return lambda i, j: (jnp.maximum(i - np_tiles, 0), col)

    in_specs = [
        pl.BlockSpec((TM, D), lambda i, j: (i, 0)),
        pl.BlockSpec((1, D), lambda i, j: (0, 0)),
        pl.BlockSpec((1, 1, D), p_idx(0)),
        pl.BlockSpec((1, 1, D), p_idx(1)),
        pl.BlockSpec((TM, D), s_idx(0)),
        pl.BlockSpec((TM, D), s_idx(1)),
        pl.BlockSpec((D, tn), lambda i, j: (0, j)),
    ]
    args = [x_all, norm_g.reshape(1, D), mod_p, mod_p, mod_s, mod_s, w]
    out_specs = pl.BlockSpec((TM, tn), lambda i, j: (i, j))
    out_shape = jax.ShapeDtypeStruct((N_TOK, n), F32)
    with_gk = gk_weights is not None
    if with_gk:
        wg, wu, bg = gk_weights
        in_specs += [
            pl.BlockSpec(wg.shape, lambda i, j: (0, 0)),
            pl.BlockSpec(wu.shape, lambda i, j: (0, 0)),
            pl.BlockSpec(bg.shape, lambda i, j: (0, 0)),
        ]
        args += [wg, wu, bg]
        out_specs = [out_specs, pl.BlockSpec((TM, GLA_HK), lambda i, j: (i, 0))]
        out_shape = [out_shape, jax.ShapeDtypeStruct((N_TOK, GLA_HK), F32)]
    return pl.pallas_call(
        functools.partial(_proj_kernel, np_tiles=np_tiles, with_gk=with_gk),
        grid=grid,
        in_specs=in_specs,
        out_specs=out_specs,
        out_shape=out_shape,
        scratch_shapes=[pltpu.VMEM((TM, D), BF16)],
        compiler_params=_params(("arbitrary", "arbitrary")),
        name="norm_proj_gk" if with_gk else "norm_proj",
    )(*args)


def _out_kernel(ap_ref, as_ref, w_ref, x_ref, gp_ref, gs_ref, *rest, np_tiles, final):
    if final:
        fg_ref, o_ref, y_ref = rest
    else:
        (o_ref,) = rest
    i = pl.program_id(0)
    a = jnp.where(i < np_tiles, ap_ref[...], as_ref[...])
    y = _dot(a, w_ref[...])
    gate = _mod_rows(i, np_tiles, gp_ref, gs_ref)
    xn = x_ref[...] + gate * y
    o_ref[...] = xn
    if final:
        ms = jnp.mean(xn * xn, axis=-1, keepdims=True)
        y_ref[...] = xn * lax.rsqrt(ms + EPS) * fg_ref[...]


def out_proj(a_p, a_s, w_out, x_all, mod_p, mod_s, final_g=None):
    tm = TM_OUT
    np_tiles = NP_TOK // tm
    tiles_per_seq = SEQ // tm
    final = final_g is not None
    in_specs = [
        pl.BlockSpec((tm, D), lambda i: (jnp.minimum(i, np_tiles - 1), 0)),
        pl.BlockSpec((tm, D), lambda i: (jnp.maximum(i - np_tiles, 0), 0)),
        pl.BlockSpec((D, D), lambda i: (0, 0)),
        pl.BlockSpec((tm, D), lambda i: (i, 0)),
        pl.BlockSpec((1, 1, D), lambda i: (jnp.minimum(i // tiles_per_seq, BATCH - 1), 0, 2)),
        pl.BlockSpec((tm, D), lambda i: (jnp.maximum(i - np_tiles, 0), 2)),
    ]
    args = [a_p, a_s, w_out, x_all, mod_p, mod_s]
    out_specs = pl.BlockSpec((tm, D), lambda i: (i, 0))
    out_shape = jax.ShapeDtypeStruct((N_TOK, D), F32)
    if final:
        in_specs.append(pl.BlockSpec((1, D), lambda i: (0, 0)))
        args.append(final_g.reshape(1, D))
        out_specs = [out_specs, pl.BlockSpec((tm, D), lambda i: (i, 0))]
        out_shape = [out_shape, jax.ShapeDtypeStruct((N_TOK, D), F32)]
    return pl.pallas_call(
        functools.partial(_out_kernel, np_tiles=np_tiles, final=final),
        grid=(N_TOK // tm,),
        in_specs=in_specs,
        out_specs=out_specs,
        out_shape=out_shape,
        compiler_params=_params(("arbitrary",)),
        name="out_proj_final" if final else "out_proj",
    )(*args)


def _cumsum_rows(x):
    n = x.shape[0]
    row = lax.broadcasted_iota(jnp.int32, x.shape, 0)
    s = 1
    while s < n:
        x = x + jnp.where(row >= s, pltpu.roll(x, s, 0), 0.0)
        s *= 2
    return x


def _col_from_row(v):
    return jnp.broadcast_to(v, (128, v.shape[1])).T


def _head_rmsnorm_gate(o, g, og):
    ms = jnp.mean(o * o, axis=-1, keepdims=True)
    return (o * lax.rsqrt(ms + EPS) * g) * _silu(og)


def _gla_prompt_kernel(q_ref, k_ref, v_ref, og_ref, gk_ref, g_ref, a_ref, s_ref):
    c = pl.program_id(2)
    C, SUB = GLA_C, GLA_SUB
    nsub = C // SUB

    @pl.when(c == 0)
    def _():
        s_ref[...] = jnp.zeros_like(s_ref)

    q = q_ref[...] * (GLA_DK ** -0.5)
    k = k_ref[...]
    v = v_ref[...].astype(BF16)
    gk = gk_ref[...]
    S = s_ref[0, 0]

    cum = _cumsum_rows(gk)
    excl = cum - gk
    last = cum[C - 1:C, :]

    o = _dot((q * jnp.exp(cum)).astype(BF16), S.astype(BF16))

    row_blocks = []
    for i in range(nsub):
        sl = slice(i * SUB, (i + 1) * SUB)
        if i == 0:
            row_blocks.append(jnp.zeros((SUB, C), F32))
            continue
        b_i = excl[i * SUB:i * SUB + 1, :]
        qt = (q[sl] * jnp.exp(cum[sl] - b_i)).astype(BF16)
        kh = (k * jnp.exp(jnp.minimum(b_i - cum, 0.0))).astype(BF16)
        att = _dot_nt(qt, kh)
        col = lax.broadcasted_iota(jnp.int32, (SUB, C), 1)
        row_blocks.append(jnp.where(col < i * SUB, att, 0.0))
    att_off = jnp.concatenate(row_blocks, axis=0)

    parts = []
    for d in range(SUB):
        k_d = k if d == 0 else pltpu.roll(k, d, 0)
        cum_d = cum if d == 0 else pltpu.roll(cum, d, 0)
        parts.append((q * k_d * jnp.exp(jnp.minimum(cum - cum_d, 0.0))).astype(BF16))
    ones = jnp.ones((GLA_DK, 128), BF16)
    band = _dot(jnp.concatenate(parts, axis=0), ones)
    r = lax.broadcasted_iota(jnp.int32, (C, C), 0)
    cc = lax.broadcasted_iota(jnp.int32, (C, C), 1)
    off = jnp.where((r & -SUB) == (cc & -SUB), r - cc, -1)
    att_band = jnp.zeros((C, C), F32)
    for d in range(SUB):
        att_band = att_band + jnp.where(off == d, band[d * C:(d + 1) * C], 0.0)

    o = o + _dot((att_off + att_band).astype(BF16), v)

    kd = (k * jnp.exp(last - cum)).astype(BF16)
    dec = _col_from_row(jnp.exp(last))
    dec = jnp.concatenate([dec] * (GLA_DV // 128), axis=1)
    s_ref[0, 0] = dec * S + _dot(kd.T, v)

    a_ref[...] = _head_rmsnorm_gate(o, g_ref[...], og_ref[...]).astype(BF16)


def gla_prompt(proj, gk, onorm_g):
    nc = SEQ // GLA_C

    def rows(b, h, c):
        return b * nc + c

    return pl.pallas_call(
        _gla_prompt_kernel,
        grid=(BATCH, GLA_H, nc),
        in_specs=[
            pl.BlockSpec((GLA_C, GLA_DK), lambda b, h, c: (rows(b, h, c), h)),
            pl.BlockSpec((GLA_C, GLA_DK), lambda b, h, c: (rows(b, h, c), GLA_H + h)),
            pl.BlockSpec((GLA_C, GLA_DV), lambda b, h, c: (rows(b, h, c), GLA_H + h)),
            pl.BlockSpec((GLA_C, GLA_DV), lambda b, h, c: (rows(b, h, c), 2 * GLA_H + h)),
            pl.BlockSpec((GLA_C, GLA_DK), lambda b, h, c: (rows(b, h, c), h)),
            pl.BlockSpec((1, GLA_DV), lambda b, h, c: (0, 0)),
        ],
        out_specs=[
            pl.BlockSpec((GLA_C, GLA_DV), lambda b, h, c: (rows(b, h, c), h)),
            pl.BlockSpec((1, 1, GLA_DK, GLA_DV), lambda b, h, c: (b, h, 0, 0)),
        ],
        out_shape=[
            jax.ShapeDtypeStruct((NP_TOK, GLA_HV), BF16),
            jax.ShapeDtypeStruct((BATCH, GLA_H, GLA_DK, GLA_DV), F32),
        ],
        compiler_params=_params(("arbitrary", "arbitrary", "arbitrary")),
        name="gla_prompt",
    )(proj, proj, proj, proj, gk, onorm_g.reshape(1, GLA_DV))


def _gla_sample_kernel(qk_ref, v_ref, og_ref, gk_ref, g_ref, s0_ref, a_ref, s_ref):
    T = DEC_SEQ
    row = lax.broadcasted_iota(jnp.int32, (T, GLA_DV), 0)
    outs = []
    for h in range(GLA_H):
        ksl = slice(h * GLA_DK, (h + 1) * GLA_DK)
        vsl = slice(h * GLA_DV, (h + 1) * GLA_DV)
        q = qk_ref[:, ksl] * (GLA_DK ** -0.5)
        k = qk_ref[:, GLA_HK + h * GLA_DK:GLA_HK + (h + 1) * GLA_DK]
        v = v_ref[:, vsl]
        gk = gk_ref[:, ksl]
        S = s0_ref[0, h]

        cum = _cumsum_rows(gk)
        last = cum[T - 1:T, :]
        o = _dot((q * jnp.exp(cum)).astype(BF16), S.astype(BF16))
        for d in range(T):
            k_d = k if d == 0 else pltpu.roll(k, d, 0)
            cum_d = cum if d == 0 else pltpu.roll(cum, d, 0)
            v_d = v if d == 0 else pltpu.roll(v, d, 0)
            w = jnp.sum(q * k_d * jnp.exp(jnp.minimum(cum - cum_d, 0.0)), axis=-1, keepdims=True)
            o = o + jnp.where(row >= d, w * v_d, 0.0)

        kd = k * jnp.exp(last - cum)
        stacked = jnp.concatenate(
            [kd, jnp.broadcast_to(jnp.exp(last), (T, GLA_DK)), jnp.zeros((128 - 2 * T, GLA_DK), F32)], axis=0)
        st = stacked.T
        vpad = jnp.concatenate([v, jnp.zeros((128 - T, GLA_DV), F32)], axis=0)
        lane = lax.broadcasted_iota(jnp.int32, (GLA_DK, 128), 1)
        kdt = jnp.where(lane < T, st, 0.0).astype(BF16)
        s_ref[0, h] = st[:, T:T + 1] * S + _dot(kdt, vpad.astype(BF16))

        outs.append(_head_rmsnorm_gate(o, g_ref[...], og_ref[:, vsl]))
    a_ref[...] = jnp.concatenate(outs, axis=1).astype(BF16)


def gla_sample(proj, gk, onorm_g, state0):
    r0 = NP_TOK // DEC_SEQ
    return pl.pallas_call(
        _gla_sample_kernel,
        grid=(DEC_BATCH,),
        in_specs=[
            pl.BlockSpec((DEC_SEQ, 2 * GLA_HK), lambda b: (r0 + b, 0)),
            pl.BlockSpec((DEC_SEQ, GLA_HV), lambda b: (r0 + b, 1)),
            pl.BlockSpec((DEC_SEQ, GLA_HV), lambda b: (r0 + b, 2)),
            pl.BlockSpec((DEC_SEQ, GLA_HK), lambda b: (r0 + b, 0)),
            pl.BlockSpec((1, GLA_DV), lambda b: (0, 0)),
            pl.BlockSpec((1, GLA_H, GLA_DK, GLA_DV), lambda b: (b, 0, 0, 0)),
        ],
        out_specs=[
            pl.BlockSpec((DEC_SEQ, GLA_HV), lambda b: (b, 0)),
            pl.BlockSpec((1, GLA_H, GLA_DK, GLA_DV), lambda b: (b, 0, 0, 0)),
        ],
        out_shape=[
            jax.ShapeDtypeStruct((NS_TOK, GLA_HV), BF16),
            jax.ShapeDtypeStruct((DEC_BATCH, GLA_H, GLA_DK, GLA_DV), F32),
        ],
        compiler_params=_params(("arbitrary",)),
        name="gla_sample",
    )(proj, proj, proj, gk, onorm_g.reshape(1, GLA_DV), state0)


def _rope_partner(x):
    lane = lax.broadcasted_iota(jnp.int32, x.shape, 1)
    return jnp.where((lane & (MLA_ROPE - 1)) < MLA_ROPE // 2, pltpu.roll(x, 128 - MLA_ROPE // 2, 1),
                     pltpu.roll(x, MLA_ROPE // 2, 1))


def _kvprep_kernel(ckv_ref, kr_ref, g_ref, inv_ref, ckvn_ref, krr_ref, c4_ref, s4_ref, *, np_tiles):
    i = pl.program_id(0)
    x = ckv_ref[...]
    ms = jnp.mean(x * x, axis=-1, keepdims=True)
    ckvn_ref[...] = x * lax.rsqrt(ms + EPS) * g_ref[...]

    r = i * TM + lax.broadcasted_iota(jnp.int32, (TM, 128), 0)
    pos = jnp.where(i < np_tiles, r & (SEQ - 1), PAST + (r & (DEC_SEQ - 1))).astype(F32)
    ang = pos * inv_ref[...]
    lane = lax.broadcasted_iota(jnp.int32, (TM, 128), 1)
    c4 = jnp.cos(ang)
    s4 = jnp.where((lane & (MLA_ROPE - 1)) < MLA_ROPE // 2, -jnp.sin(ang), jnp.sin(ang))
    c4_ref[...] = c4
    s4_ref[...] = s4
    kr = kr_ref[...]
    krr_ref[...] = (kr * c4 + _rope_partner(kr) * s4)[:, :MLA_ROPE]


def mla_kvprep(proj, kv_norm_g):
    inv = ROPE_BASE ** (-np.arange(0, MLA_ROPE, 2, dtype=np.float32) / MLA_ROPE)
    inv4 = jnp.asarray(np.tile(inv.astype(np.float32), 4).reshape(1, 128))
    np_tiles = NP_TOK // TM
    return pl.pallas_call(
        functools.partial(_kvprep_kernel, np_tiles=np_tiles),
        grid=(N_TOK // TM,),
        in_specs=[
            pl.BlockSpec((TM, MLA_RANK), lambda i: (i, 3072 // MLA_RANK)),
            pl.BlockSpec((TM, 128), lambda i: (i, 5632 // 128)),
            pl.BlockSpec((1, MLA_RANK), lambda i: (0, 0)),
            pl.BlockSpec((1, 128), lambda i: (0, 0)),
        ],
        out_specs=[
            pl.BlockSpec((TM, MLA_RANK), lambda i: (i, 0)),
            pl.BlockSpec((TM, MLA_ROPE), lambda i: (i, 0)),
            pl.BlockSpec((TM, 128), lambda i: (i, 0)),
            pl.BlockSpec((TM, 128), lambda i: (i, 0)),
        ],
        out_shape=[
            jax.ShapeDtypeStruct((N_TOK, MLA_RANK), F32),
            jax.ShapeDtypeStruct((N_TOK, MLA_ROPE), F32),
            jax.ShapeDtypeStruct((N_TOK, 128), F32),
            jax.ShapeDtypeStruct((N_TOK, 128), F32),
        ],
        compiler_params=_params(("arbitrary",)),
        name="mla_kvprep",
    )(proj, proj, kv_norm_g.reshape(1, MLA_RANK), inv4)


def _kvup_kernel(ckv_ref, kr_ref, wuk_ref, wuv_ref, k_ref, v_ref):
    c = ckv_ref[...].astype(BF16)
    k_ref[0, :, :MLA_NOPE] = _dot(c, wuk_ref[0]).astype(BF16)
    k_ref[0, :, MLA_NOPE:] = kr_ref[...].astype(BF16)
    v_ref[0] = _dot(c, wuv_ref[0]).astype(BF16)


def mla_kvup(ckv_n, kr_r, w_uk3, w_uv3):
    return pl.pallas_call(
        _kvup_kernel,
        grid=(NP_TOK // TM, MLA_H),
        in_specs=[
            pl.BlockSpec((TM, MLA_RANK), lambda i, h: (i, 0)),
            pl.BlockSpec((TM, MLA_ROPE), lambda i, h: (i, 0)),
            pl.BlockSpec((1, MLA_RANK, MLA_NOPE), lambda i, h: (h, 0, 0)),
            pl.BlockSpec((1, MLA_RANK, MLA_VH), lambda i, h: (h, 0, 0)),
        ],
        out_specs=[
            pl.BlockSpec((1, TM, MLA_QK), lambda i, h: (h, i, 0)),
            pl.BlockSpec((1, TM, MLA_VH), lambda i, h: (h, i, 0)),
        ],
        out_shape=[
            jax.ShapeDtypeStruct((MLA_H, NP_TOK, MLA_QK), BF16),
            jax.ShapeDtypeStruct((MLA_H, NP_TOK, MLA_VH), BF16),
        ],
        compiler_params=_params(("arbitrary", "arbitrary")),
        name="mla_kvup",
    )(ckv_n, kr_r, w_uk3, w_uv3)


FLASH_T = 512


def _rope_pair_select(x2, h):
    return jnp.where(h % 2 == 1, pltpu.roll(x2, MLA_ROPE, 1), x2)[:, :MLA_ROPE]


def _flash_kernel(qn_ref, qr_ref, c4_ref, s4_ref, k_ref, v_ref, gate_ref, o_ref, q_sc, m_sc, l_sc, acc_sc):
    h = pl.program_id(1)
    qi = pl.program_id(2)
    ki = pl.program_id(3)
    T = FLASH_T

    @pl.when(ki == 0)
    def _():
        q_sc[:, :MLA_NOPE] = (qn_ref[...] * MLA_SCALE).astype(BF16)
        x2 = qr_ref[...]
        rot = x2 * c4_ref[...] + _rope_partner(x2) * s4_ref[...]
        q_sc[:, MLA_NOPE:] = (_rope_pair_select(rot, h) * MLA_SCALE).astype(BF16)
        m_sc[...] = jnp.full_like(m_sc, _NEG)
        l_sc[...] = jnp.zeros_like(l_sc)
        acc_sc[...] = jnp.zeros_like(acc_sc)

    def step(masked):
        s = _dot_nt(q_sc[...], k_ref[0])
        if masked:
            r = lax.broadcasted_iota(jnp.int32, (T, T), 0)
            c = lax.broadcasted_iota(jnp.int32, (T, T), 1)
            s = jnp.where(c <= r, s, _NEG)
        m_prev = m_sc[...]
        m_new = jnp.maximum(m_prev, jnp.max(s, axis=-1, keepdims=True))
        alpha = jnp.exp(m_prev - m_new)
        p = jnp.exp(s - m_new)
        l_sc[...] = alpha * l_sc[...] + jnp.sum(p, axis=-1, keepdims=True)
        acc_sc[...] = alpha * acc_sc[...] + _dot(p.astype(BF16), v_ref[0])
        m_sc[...] = m_new

    @pl.when(ki < qi)
    def _():
        step(False)

    @pl.when(ki == qi)
    def _():
        step(True)
        o = acc_sc[...] / l_sc[...]
        o_ref[...] = (o * _silu(gate_ref[...])).astype(BF16)


def mla_flash(proj, c4, s4, kcat, vv):
    T = FLASH_T
    nq = SEQ // T

    def qrow(b, h, qi, ki):
        return b * nq + qi

    def krow(b, h, qi, ki):
        return b * nq + jnp.minimum(ki, qi)

    return pl.pallas_call(
        _flash_kernel,
        grid=(BATCH, MLA_H, nq, nq),
        in_specs=[
            pl.BlockSpec((T, 128), lambda b, h, qi, ki: (qrow(b, h, qi, ki), h)),
            pl.BlockSpec((T, 128), lambda b, h, qi, ki: (qrow(b, h, qi, ki), 2048 // 128 + h // 2)),
            pl.BlockSpec((T, 128), lambda b, h, qi, ki: (qrow(b, h, qi, ki), 0)),
            pl.BlockSpec((T, 128), lambda b, h, qi, ki: (qrow(b, h, qi, ki), 0)),
            pl.BlockSpec((1, T, MLA_QK), lambda b, h, qi, ki: (h, krow(b, h, qi, ki), 0)),
            pl.BlockSpec((1, T, MLA_VH), lambda b, h, qi, ki: (h, krow(b, h, qi, ki), 0)),
            pl.BlockSpec((T, 128), lambda b, h, qi, ki: (qrow(b, h, qi, ki), 3584 // 128 + h)),
        ],
        out_specs=pl.BlockSpec((T, 128), lambda b, h, qi, ki: (qrow(b, h, qi, ki), h)),
        out_shape=jax.ShapeDtypeStruct((NP_TOK, MLA_H * MLA_VH), BF16),
        scratch_shapes=[
            pltpu.VMEM((T, MLA_QK), BF16),
            pltpu.VMEM((T, 1), F32),
            pltpu.VMEM((T, 1), F32),
            pltpu.VMEM((T, MLA_VH), F32),
        ],
        compiler_params=_params(("arbitrary", "arbitrary", "arbitrary", "arbitrary")),
        name="mla_flash",
    )(proj, proj, c4, s4, kcat, vv, proj)


def _qabs_kernel(qn_ref, qr_ref, c4_ref, s4_ref, wukt_ref, o_ref):
    h = pl.program_id(0)
    ql = _dot(qn_ref[...].astype(BF16), wukt_ref[0]) * MLA_SCALE
    x2 = qr_ref[...]
    rot = x2 * c4_ref[...] + _rope_partner(x2) * s4_ref[...]
    qr = _rope_pair_select(rot, h) * MLA_SCALE
    o_ref[:, 0, :, :MLA_RANK] = ql.reshape(DEC_BATCH, DEC_SEQ, MLA_RANK)
    o_ref[:, 0, :, MLA_RANK:] = qr.reshape(DEC_BATCH, DEC_SEQ, MLA_ROPE)


def mla_qabs(proj, c4, s4, w_ukt3):
    rb = NP_TOK // NS_TOK
    return pl.pallas_call(
        _qabs_kernel,
        grid=(MLA_H,),
        in_specs=[
            pl.BlockSpec((NS_TOK, 128), lambda h: (rb, h)),
            pl.BlockSpec((NS_TOK, 128), lambda h: (rb, 2048 // 128 + h // 2)),
            pl.BlockSpec((NS_TOK, 128), lambda h: (rb, 0)),
            pl.BlockSpec((NS_TOK, 128), lambda h: (rb, 0)),
            pl.BlockSpec((1, MLA_NOPE, MLA_RANK), lambda h: (h, 0, 0)),
        ],
        out_specs=pl.BlockSpec((DEC_BATCH, 1, DEC_SEQ, MLA_LAT), lambda h: (0, h, 0, 0)),
        out_shape=jax.ShapeDtypeStruct((DEC_BATCH, MLA_H, DEC_SEQ, MLA_LAT), F32),
        compiler_params=_params(("arbitrary",)),
        name="mla_qabs",
    )(proj, proj, c4, s4, w_ukt3)


DEC_PAGES = 16
DEC_STEPS = N_PAGES // DEC_PAGES


def _decode_kernel(pt_ref, q_ref, *rest):
    ckv_refs = rest[:DEC_PAGES]
    kr_refs = rest[DEC_PAGES:2 * DEC_PAGES]
    cnew_ref, knew_ref, o_ref, kv_sc, kr_sc, m_sc, l_sc, acc_sc = rest[2 * DEC_PAGES:]
    s_id = pl.program_id(1)
    R = MLA_H * DEC_SEQ

    @pl.when(s_id == 0)
    def _():
        m_sc[...] = jnp.full_like(m_sc, _NEG)
        l_sc[...] = jnp.zeros_like(l_sc)
        acc_sc[...] = jnp.zeros_like(acc_sc)

    q = q_ref[0].reshape(R, MLA_LAT)
    ql = q[:, :MLA_RANK].astype(BF16)
    qr = q[:, MLA_RANK:].astype(BF16)

    def online(s, vals):
        m_prev = m_sc[...]
        m_new = jnp.maximum(m_prev, jnp.max(s, axis=-1, keepdims=True))
        alpha = jnp.exp(m_prev - m_new)
        p = jnp.exp(s - m_new)
        l_sc[...] = alpha * l_sc[...] + jnp.sum(p, axis=-1, keepdims=True)
        acc_sc[...] = alpha * acc_sc[...] + _dot(p.astype(BF16), vals)
        m_sc[...] = m_new

    @pl.when(s_id < DEC_STEPS)
    def _():
        for r in range(DEC_PAGES):
            kv_sc[r * PAGE:(r + 1) * PAGE, :] = ckv_refs[r][0, 0].astype(BF16)
            kr_sc[r * PAGE:(r + 1) * PAGE, :] = kr_refs[r][0, 0].astype(BF16)
        kv = kv_sc[...]
        online(_dot_nt(ql, kv) + _dot_nt(qr, kr_sc[...]), kv)

    @pl.when(s_id == DEC_STEPS)
    def _():
        cn = jnp.concatenate([cnew_ref[...], jnp.zeros((PAGE - DEC_SEQ, MLA_RANK), F32)], axis=0).astype(BF16)
        kn = jnp.concatenate([knew_ref[...], jnp.zeros((PAGE - DEC_SEQ, MLA_ROPE), F32)], axis=0).astype(BF16)
        s = _dot_nt(ql, cn) + _dot_nt(qr, kn)
        t = lax.broadcasted_iota(jnp.int32, (R, PAGE), 0) & (DEC_SEQ - 1)
        j = lax.broadcasted_iota(jnp.int32, (R, PAGE), 1)
        online(jnp.where(j <= t, s, _NEG), cn)
        o = acc_sc[...] / l_sc[...]
        o_ref[0] = o.reshape(MLA_H, DEC_SEQ, MLA_RANK)


def mla_decode(qcat, cache_ckv, cache_kr, layer, page_table, ckv_n, kr_r):
    r0 = NP_TOK // DEC_SEQ

    def page_idx(r):
        return lambda b, s, pt: (layer, pt[b * N_PAGES + jnp.minimum(s, DEC_STEPS - 1) * DEC_PAGES + r], 0, 0)

    in_specs = [pl.BlockSpec((1, MLA_H, DEC_SEQ, MLA_LAT), lambda b, s, pt: (b, 0, 0, 0))]
    in_specs += [pl.BlockSpec((1, 1, PAGE, MLA_RANK), page_idx(r)) for r in range(DEC_PAGES)]
    in_specs += [pl.BlockSpec((1, 1, PAGE, MLA_ROPE), page_idx(r)) for r in range(DEC_PAGES)]
    in_specs += [
        pl.BlockSpec((DEC_SEQ, MLA_RANK), lambda b, s, pt: (r0 + b, 0)),
        pl.BlockSpec((DEC_SEQ, MLA_ROPE), lambda b, s, pt: (r0 + b, 0)),
    ]

    grid_spec = pltpu.PrefetchScalarGridSpec(
        num_scalar_prefetch=1,
        grid=(DEC_BATCH, DEC_STEPS + 1),
        in_specs=in_specs,
        out_specs=pl.BlockSpec((1, MLA_H, DEC_SEQ, MLA_RANK), lambda b, s, pt: (b, 0, 0, 0)),
        scratch_shapes=[
            pltpu.VMEM((DEC_PAGES * PAGE, MLA_RANK), BF16),
            pltpu.VMEM((DEC_PAGES * PAGE, MLA_ROPE), BF16),
            pltpu.VMEM((MLA_H * DEC_SEQ, 1), F32),
            pltpu.VMEM((MLA_H * DEC_SEQ, 1), F32),
            pltpu.VMEM((MLA_H * DEC_SEQ, MLA_RANK), F32),
        ],
    )
    args = [page_table.reshape(-1), qcat] + [cache_ckv] * DEC_PAGES + [cache_kr] * DEC_PAGES + [ckv_n, kr_r]
    return pl.pallas_call(
        _decode_kernel,
        grid_spec=grid_spec,
        out_shape=jax.ShapeDtypeStruct((DEC_BATCH, MLA_H, DEC_SEQ, MLA_RANK), F32),
        compiler_params=_params(("arbitrary", "arbitrary")),
        name="mla_decode",
    )(*args)


def _uvup_kernel(ol_ref, wuv_ref, gate_ref, a_ref):
    ol = ol_ref[...].reshape(NS_TOK, MLA_RANK).astype(BF16)
    a_ref[...] = (_dot(ol, wuv_ref[0]) * _silu(gate_ref[...])).astype(BF16)


def mla_uvup(o_lat, w_uv3, proj):
    rb = NP_TOK // NS_TOK
    return pl.pallas_call(
        _uvup_kernel,
        grid=(MLA_H,),
        in_specs=[
            pl.BlockSpec((DEC_BATCH, 1, DEC_SEQ, MLA_RANK), lambda h: (0, h, 0, 0)),
            pl.BlockSpec((1, MLA_RANK, MLA_VH), lambda h: (h, 0, 0)),
            pl.BlockSpec((NS_TOK, 128), lambda h: (rb, 3584 // 128 + h)),
        ],
        out_specs=pl.BlockSpec((NS_TOK, MLA_VH), lambda h: (0, h)),
        out_shape=jax.ShapeDtypeStruct((NS_TOK, MLA_H * MLA_VH), BF16),
        compiler_params=_params(("arbitrary",)),
        name="mla_uvup",
    )(o_lat, w_uv3, proj)


POOL_HALO = 16


def _pool_prompt_kernel(u_ref, halo_ref, gate_ref, wg_ref, sc_ref, a_ref, ext_sc, *, tiles_per_seq):
    i = pl.program_id(0)
    g = pl.program_id(1)
    first = (i % tiles_per_seq) == 0
    u = u_ref[...]
    ext_sc[:POOL_HALO, :] = jnp.where(first, 0.0, halo_ref[...])
    ext_sc[POOL_HALO:, :] = u
    t = (i % tiles_per_seq) * TM + lax.broadcasted_iota(jnp.int32, (TM, 1), 0)

    for gi, w in enumerate(POOL_WINDOWS):
        @pl.when(g == gi)
        def _(w=w):
            acc = u
            for j in range(1, w):
                acc = acc + ext_sc[POOL_HALO - j:POOL_HALO - j + TM, :]
            cnt = jnp.minimum(t + 1, w).astype(F32)
            p = (acc / cnt - u).astype(BF16)
            z = _dot(p, wg_ref[0]) * sc_ref[...]
            a_ref[...] = (z * _silu(gate_ref[...])).astype(BF16)


def pool_prompt(proj, w_grp, pscale):
    tiles_per_seq = SEQ // TM
    hb = TM // POOL_HALO
    return pl.pallas_call(
        functools.partial(_pool_prompt_kernel, tiles_per_seq=tiles_per_seq),
        grid=(NP_TOK // TM, len(POOL_WINDOWS)),
        in_specs=[
            pl.BlockSpec((TM, POOL_G), lambda i, g: (i, g)),
            pl.BlockSpec((POOL_HALO, POOL_G), lambda i, g: (jnp.maximum(i * hb - 1, 0), g)),
            pl.BlockSpec((TM, POOL_G), lambda i, g: (i, len(POOL_WINDOWS) + g)),
            pl.BlockSpec((1, POOL_G, POOL_G), lambda i, g: (g, 0, 0)),
            pl.BlockSpec((1, POOL_G), lambda i, g: (0, g)),
        ],
        out_specs=pl.BlockSpec((TM, POOL_G), lambda i, g: (i, g)),
        out_shape=jax.ShapeDtypeStruct((NP_TOK, D), BF16),
        scratch_shapes=[pltpu.VMEM((POOL_HALO + TM, POOL_G), F32)],
        compiler_params=_params(("arbitrary", "arbitrary")),
        name="pool_prompt",
    )(proj, proj, proj, w_grp, pscale)


def _pool_sample_kernel(u_ref, gate_ref, hist_ref, wg_ref, sc_ref, a_ref):
    g = pl.program_id(0)

    def seq(r):
        return hist_ref[r] if r < POOL_HIST else u_ref[r - POOL_HIST]

    for gi, w in enumerate(POOL_WINDOWS):
        @pl.when(g == gi)
        def _(w=w):
            ps = []
            for t in range(DEC_SEQ):
                acc = seq(POOL_HIST + t)
                for j in range(1, w):
                    acc = acc + seq(POOL_HIST + t - j)
                ps.append(acc / float(w) - u_ref[t])
            p = jnp.concatenate(ps, axis=0).astype(BF16)
            z = _dot(p, wg_ref[0]) * sc_ref[...]
            gate = gate_ref[...].reshape(NS_TOK, POOL_G)
            a_ref[...] = (z * _silu(gate)).astype(BF16).reshape(DEC_SEQ, DEC_BATCH, POOL_G)


def pool_sample(proj_t, hist_t, w_grp, pscale):
    ng = len(POOL_WINDOWS)
    return pl.pallas_call(
        _pool_sample_kernel,
        grid=(ng,),
        in_specs=[
            pl.BlockSpec((DEC_SEQ, DEC_BATCH, POOL_G), lambda g: (0, 0, g)),
            pl.BlockSpec((DEC_SEQ, DEC_BATCH, POOL_G), lambda g: (0, 0, ng + g)),
            pl.BlockSpec((POOL_HIST, DEC_BATCH, POOL_G), lambda g: (0, 0, g)),
            pl.BlockSpec((1, POOL_G, POOL_G), lambda g: (g, 0, 0)),
            pl.BlockSpec((1, POOL_G), lambda g: (0, g)),
        ],
        out_specs=pl.BlockSpec((DEC_SEQ, DEC_BATCH, POOL_G), lambda g: (0, 0, g)),
        out_shape=jax.ShapeDtypeStruct((DEC_SEQ, DEC_BATCH, D), BF16),
        compiler_params=_params(("arbitrary",)),
        name="pool_sample",
    )(proj_t, proj_t, hist_t, w_grp, pscale)


def kernel(x_prompt, x_sample, c_prompt, c_sample, state_gla, cache_ckv, cache_kr, state_pool, page_table, norm_g, ada_w, ada_b, final_norm_g, gla_w_in, gla_w_gate_up, gla_b_gate, gla_onorm_g, gla_w_out, mla_w_in, mla_kv_norm_g, mla_w_uk, mla_w_uv, mla_w_out, pool_w_in, pool_w_grp, pool_scale, pool_w_out):
    x_all = jnp.concatenate([x_prompt.reshape(NP_TOK, D), x_sample.reshape(NS_TOK, D)], axis=0)

    n_c = BATCH + DEC_BATCH
    c_all = jnp.concatenate([c_prompt, c_sample, jnp.zeros((8 - n_c % 8, D), F32)], axis=0)
    mod = ada_mod(c_all, ada_w, ada_b)

    gla_states_p, gla_states_s = [], []
    ckv_rows, kr_rows, pool_p, pool_s = [], [], [], []
    ia = ib = ic = 0
    y_all = None
    for l in range(DEPTH):
        mod_p = mod[l, :BATCH].reshape(BATCH, 1, 3 * D)
        mod_s = jnp.repeat(mod[l, BATCH:n_c], DEC_SEQ, axis=0)
        mixer = LAYER_MIXER[l]
        if mixer == 0:
            w_in = gla_w_in[ia]
            cut = 2 * GLA_HK + GLA_HV
            w_main = jnp.concatenate([w_in[:, :cut], w_in[:, cut + GLA_RANK:]], axis=1).astype(BF16)
            w_glr = jnp.pad(w_in[:, cut:cut + GLA_RANK], ((0, 0), (0, 128 - GLA_RANK))).astype(BF16)
            w_up = jnp.pad(gla_w_gate_up[ia], ((0, 128 - GLA_RANK), (0, 0))).astype(BF16)
            proj, gk = norm_proj(x_all, norm_g[l], mod_p, mod_s, w_main, 1024,
                                 gk_weights=(w_glr, w_up, gla_b_gate[ia].reshape(1, GLA_HK)))
            a_p, st_p = gla_prompt(proj, gk, gla_onorm_g[ia])
            a_s, st_s = gla_sample(proj, gk, gla_onorm_g[ia], state_gla[ia])
            gla_states_p.append(st_p)
            gla_states_s.append(st_s)
            w_out = gla_w_out[ia].astype(BF16)
            ia += 1
        elif mixer == 1:
            w_in = mla_w_in[ib]
            nq = MLA_H * MLA_QK
            wq = w_in[:, :nq].reshape(D, MLA_H, MLA_QK)
            w_perm = jnp.concatenate([
                wq[:, :, :MLA_NOPE].reshape(D, MLA_H * MLA_NOPE),
                wq[:, :, MLA_NOPE:].reshape(D, MLA_H * MLA_ROPE),
                w_in[:, nq:nq + MLA_RANK],
                w_in[:, nq + MLA_RANK + MLA_ROPE:],
                w_in[:, nq + MLA_RANK:nq + MLA_RANK + MLA_ROPE],
                jnp.zeros((D, 128 - MLA_ROPE), F32),
            ], axis=1).astype(BF16)
            proj = norm_proj(x_all, norm_g[l], mod_p, mod_s, w_perm, 1152)
            ckv_n, kr_r, c4, s4 = mla_kvprep(proj, mla_kv_norm_g[ib])
            w_uk3 = jnp.transpose(mla_w_uk[ib], (1, 0, 2)).astype(BF16)
            w_ukt3 = jnp.transpose(mla_w_uk[ib], (1, 2, 0)).astype(BF16)
            w_uv3 = jnp.transpose(mla_w_uv[ib], (1, 0, 2)).astype(BF16)
            kcat, vv = mla_kvup(ckv_n, kr_r, w_uk3, w_uv3)
            a_p = mla_flash(proj, c4, s4, kcat, vv)
            qcat = mla_qabs(proj, c4, s4, w_ukt3)
            o_lat = mla_decode(qcat, cache_ckv, cache_kr, ib, page_table, ckv_n, kr_r)
            a_s = mla_uvup(o_lat, w_uv3, proj)
            ckv_rows.append(ckv_n)
            kr_rows.append(kr_r)
            w_out = mla_w_out[ib].astype(BF16)
            ib += 1
        else:
            proj = norm_proj(x_all, norm_g[l], mod_p, mod_s, pool_w_in[ic].astype(BF16), 1024)
            w_grp = pool_w_grp[ic].astype(BF16)
            pscale = pool_scale[ic].reshape(1, D)
            a_p = pool_prompt(proj, w_grp, pscale)
            proj_t = jnp.transpose(proj[NP_TOK:].reshape(DEC_BATCH, DEC_SEQ, 2 * D), (1, 0, 2))
            hist_t = jnp.transpose(state_pool[ic], (1, 0, 2))
            a_st = pool_sample(proj_t, hist_t, w_grp, pscale)
            a_s = jnp.transpose(a_st, (1, 0, 2)).reshape(NS_TOK, D)
            u_p = proj[:NP_TOK, :D].reshape(BATCH, SEQ, D)
            u_s = proj[NP_TOK:, :D].reshape(DEC_BATCH, DEC_SEQ, D)
            pool_p.append(u_p[:, SEQ - POOL_HIST:, :])
            pool_s.append(jnp.concatenate([state_pool[ic][:, DEC_SEQ:, :], u_s], axis=1))
            w_out = pool_w_out[ic].astype(BF16)
            ic += 1
        if l == DEPTH - 1:
            x_all, y_all = out_proj(a_p, a_s, w_out, x_all, mod_p, mod_s, final_g=final_norm_g)
        else:
            x_all = out_proj(a_p, a_s, w_out, x_all, mod_p, mod_s)

    y_prompt = y_all[:NP_TOK].reshape(BATCH, SEQ, D)
    y_sample = y_all[NP_TOK:].reshape(DEC_BATCH, DEC_SEQ, D)
    ckv_all = jnp.stack(ckv_rows)
    kr_all = jnp.stack(kr_rows)
    return (
        y_prompt,
        y_sample,
        jnp.stack(gla_states_p),
        jnp.stack(gla_states_s),
        ckv_all[:, :NP_TOK].reshape(-1, BATCH, SEQ, MLA_RANK),
        kr_all[:, :NP_TOK].reshape(-1, BATCH, SEQ, MLA_ROPE),
        ckv_all[:, NP_TOK:].reshape(-1, DEC_BATCH, DEC_SEQ, MLA_RANK),
        kr_all[:, NP_TOK:].reshape(-1, DEC_BATCH, DEC_SEQ, MLA_ROPE),
        jnp.stack(pool_p),
        jnp.stack(pool_s),
    )
```

```python
import functools

import numpy as np
import jax
import jax.numpy as jnp
from jax import lax
from jax.experimental import pallas as pl
from jax.experimental.pallas import tpu as pltpu

F32 = jnp.float32
BF16 = jnp.bfloat16

D = 2048
BATCH = 4
SEQ = 2048
DEC_BATCH = 128
DEC_SEQ = 8
PAGE = 128
N_PAGES = 64
PAST = N_PAGES * PAGE
DEPTH = 4
LAYER_MIXER = (0, 1, 2, 0)
EPS = 1e-6

NP_TOK = BATCH * SEQ
NS_TOK = DEC_BATCH * DEC_SEQ
N_TOK = NP_TOK + NS_TOK

GLA_H = 4
GLA_DK = 256
GLA_DV = 512
GLA_RANK = 16
GLA_TAU = 16.0
GLA_HK = GLA_H * GLA_DK
GLA_HV = GLA_H * GLA_DV
GLA_C = 128
GLA_SUB = 16

MLA_H = 16
MLA_NOPE = 128
MLA_ROPE = 64
MLA_VH = 128
MLA_RANK = 512
MLA_SCALE = (MLA_NOPE + MLA_ROPE) ** -0.5
MLA_QK = MLA_NOPE + MLA_ROPE
MLA_LAT = MLA_RANK + MLA_ROPE
MLA_N = 5760
ROPE_BASE = 10000.0

POOL_WINDOWS = (2, 4, 8, 16)
POOL_G = 512
POOL_HIST = 15

TM = 512
TM_OUT = 256
VMEM_LIMIT = 56 * 1024 * 1024

_NEG = -1e30


def _params(sem):
    return pltpu.CompilerParams(dimension_semantics=sem, vmem_limit_bytes=VMEM_LIMIT)


def _silu(x):
    return x * (1.0 / (1.0 + jnp.exp(-x)))


def _dot(a, b):
    return jnp.dot(a, b, preferred_element_type=F32)


def _dot_nt(a, b):
    return lax.dot_general(a, b, (((1,), (1,)), ((), ())), preferred_element_type=F32)


def _ada_kernel(c_ref, w_ref, b_ref, o_ref):
    sc = _silu(c_ref[...]).astype(BF16)
    o_ref[0] = _dot(sc, w_ref[0].astype(BF16)) + b_ref[0]


def ada_mod(c_all, ada_w, ada_b):
    rows = c_all.shape[0]
    tn = 768
    return pl.pallas_call(
        _ada_kernel,
        grid=(DEPTH, 3 * D // tn),
        in_specs=[
            pl.BlockSpec((rows, D), lambda l, j: (0, 0)),
            pl.BlockSpec((1, D, tn), lambda l, j: (l, 0, j)),
            pl.BlockSpec((1, 1, tn), lambda l, j: (l, 0, j)),
        ],
        out_specs=pl.BlockSpec((1, rows, tn), lambda l, j: (l, 0, j)),
        out_shape=jax.ShapeDtypeStruct((DEPTH, rows, 3 * D), F32),
        compiler_params=_params(("arbitrary", "arbitrary")),
        name="ada_mod",
    )(c_all, ada_w, ada_b.reshape(DEPTH, 1, 3 * D))


def _mod_rows(i, np_tiles, p_ref, s_ref):
    return jnp.where(i < np_tiles, p_ref[0], s_ref[...])


def _proj_kernel(x_ref, g_ref, shp_ref, scp_ref, shs_ref, scs_ref, w_ref, *rest, np_tiles, with_gk):
    if with_gk:
        wg_ref, wu_ref, bg_ref, o_ref, gk_ref, h_sc = rest
    else:
        o_ref, h_sc = rest
    i = pl.program_id(0)
    j = pl.program_id(1)

    @pl.when(j == 0)
    def _():
        x = x_ref[...]
        ms = jnp.mean(x * x, axis=-1, keepdims=True)
        y = x * lax.rsqrt(ms + EPS) * g_ref[...]
        shift = _mod_rows(i, np_tiles, shp_ref, shs_ref)
        scale = _mod_rows(i, np_tiles, scp_ref, scs_ref)
        h = (y * (1.0 + scale) + shift).astype(BF16)
        h_sc[...] = h
        if with_gk:
            glr = _dot(h, wg_ref[...]).astype(BF16)
            z = _dot(glr, wu_ref[...]) + bg_ref[...]
            gk_ref[...] = (jnp.minimum(z, 0.0) - jnp.log1p(jnp.exp(-jnp.abs(z)))) / GLA_TAU

    o_ref[...] = _dot(h_sc[...], w_ref[...])


def norm_proj(x_all, norm_g, mod_p, mod_s, w, tn, gk_weights=None):
    n = w.shape[1]
    np_tiles = NP_TOK // TM
    tiles_per_seq = SEQ // TM
    grid = (N_TOK // TM, n // tn)

    def p_idx(col):
        return lambda i, j: (jnp.minimum(i // tiles_per_seq, BATCH - 1), 0, col)

    def s_idx(col):
        return lambda i, j: (jnp.maximum(i - np_tiles, 0), col)

    in_specs = [
        pl.BlockSpec((TM, D), lambda i, j: (i, 0)),
        pl.BlockSpec((1, D), lambda i, j: (0, 0)),
        pl.BlockSpec((1, 1, D), p_idx(0)),
        pl.BlockSpec((1, 1, D), p_idx(1)),
        pl.BlockSpec((TM, D), s_idx(0)),
        pl.BlockSpec((TM, D), s_idx(1)),
        pl.BlockSpec((D, tn), lambda i, j: (0, j)),
    ]
    args = [x_all, norm_g.reshape(1, D), mod_p, mod_p, mod_s, mod_s, w]
    out_specs = pl.BlockSpec((TM, tn), lambda i, j: (i, j))
    out_shape = jax.ShapeDtypeStruct((N_TOK, n), F32)
    with_gk = gk_weights is not None
    if with_gk:
        wg, wu, bg = gk_weights
        in_specs += [
            pl.BlockSpec(wg.shape, lambda i, j: (0, 0)),
            pl.BlockSpec(wu.shape, lambda i, j: (0, 0)),
            pl.BlockSpec(bg.shape, lambda i, j: (0, 0)),
        ]
        args += [wg, wu, bg]
        out_specs = [out_specs, pl.BlockSpec((TM, GLA_HK), lambda i, j: (i, 0))]
        out_shape = [out_shape, jax.ShapeDtypeStruct((N_TOK, GLA_HK), F32)]
    return pl.pallas_call(
        functools.partial(_proj_kernel, np_tiles=np_tiles, with_gk=with_gk),
        grid=grid,
        in_specs=in_specs,
        out_specs=out_specs,
        out_shape=out_shape,
        scratch_shapes=[pltpu.VMEM((TM, D), BF16)],
        compiler_params=_params(("arbitrary", "arbitrary")),
        name="norm_proj_gk" if with_gk else "norm_proj",
    )(*args)


def _out_kernel(ap_ref, as_ref, w_ref, x_ref, gp_ref, gs_ref, *rest, np_tiles, final):
    i = pl.program_id(0)
    a = jnp.where(i < np_tiles, ap_ref[...], as_ref[...])
    y = _dot(a, w_ref[...])
    gate = _mod_rows(i, np_tiles, gp_ref, gs_ref)
    xn = x_ref[...] + gate * y
    if not final:
        (o_ref,) = rest
        o_ref[...] = xn
        return
    fg_ref, yp_ref, ys_ref = rest
    ms = jnp.mean(xn * xn, axis=-1, keepdims=True)
    yn = xn * lax.rsqrt(ms + EPS) * fg_ref[...]

    @pl.when(i < np_tiles)
    def _():
        yp_ref[...] = yn

    @pl.when(i >= np_tiles)
    def _():
        ys_ref[...] = yn


def out_proj(a_p, a_s, w_out, x_all, mod_p, mod_s, final_g=None):
    tm = TM_OUT
    np_tiles = NP_TOK // tm
    tiles_per_seq = SEQ // tm
    final = final_g is not None
    in_specs = [
        pl.BlockSpec((tm, D), lambda i: (jnp.minimum(i, np_tiles - 1), 0)),
        pl.BlockSpec((tm, D), lambda i: (jnp.maximum(i - np_tiles, 0), 0)),
        pl.BlockSpec((D, D), lambda i: (0, 0)),
        pl.BlockSpec((tm, D), lambda i: (i, 0)),
        pl.BlockSpec((1, 1, D), lambda i: (jnp.minimum(i // tiles_per_seq, BATCH - 1), 0, 2)),
        pl.BlockSpec((tm, D), lambda i: (jnp.maximum(i - np_tiles, 0), 2)),
    ]
    args = [a_p, a_s, w_out, x_all, mod_p, mod_s]
    out_specs = pl.BlockSpec((tm, D), lambda i: (i, 0))
    out_shape = jax.ShapeDtypeStruct((N_TOK, D), F32)
    if final:
        in_specs.append(pl.BlockSpec((1, D), lambda i: (0, 0)))
        args.append(final_g.reshape(1, D))
        out_specs = [
            pl.BlockSpec((tm, D), lambda i: (jnp.minimum(i, np_tiles - 1), 0)),
            pl.BlockSpec((tm, D), lambda i: (jnp.maximum(i - np_tiles, 0), 0)),
        ]
        out_shape = [jax.ShapeDtypeStruct((NP_TOK, D), F32), jax.ShapeDtypeStruct((NS_TOK, D), F32)]
    return pl.pallas_call(
        functools.partial(_out_kernel, np_tiles=np_tiles, final=final),
        grid=(N_TOK // tm,),
        in_specs=in_specs,
        out_specs=out_specs,
        out_shape=out_shape,
        compiler_params=_params(("arbitrary",)),
        name="out_proj_final" if final else "out_proj",
    )(*args)


def _cumsum_rows(x):
    n = x.shape[0]
    row = lax.broadcasted_iota(jnp.int32, x.shape, 0)
    s = 1
    while s < n:
        x = x + jnp.where(row >= s, pltpu.roll(x, s, 0), 0.0)
        s *= 2
    return x


LOG2E = 1.4426950408889634


def _cumsum_mxu(x):
    n = x.shape[0]
    r = lax.broadcasted_iota(jnp.int32, (n, n), 0)
    c = lax.broadcasted_iota(jnp.int32, (n, n), 1)
    tri = jnp.where(c <= r, 1.0, 0.0).astype(BF16)
    hi = x.astype(BF16)
    rem = x - hi.astype(F32)
    mid = rem.astype(BF16)
    lo = (rem - mid.astype(F32)).astype(BF16)
    return _dot(tri, hi) + _dot(tri, mid) + _dot(tri, lo)


def _col_from_row(v):
    return jnp.broadcast_to(v, (128, v.shape[1])).T


def _head_rmsnorm_gate(o, g, og):
    ms = jnp.mean(o * o, axis=-1, keepdims=True)
    return (o * lax.rsqrt(ms + EPS) * g) * _silu(og)


def _gla_prompt_kernel(q_ref, k_ref, v_ref, og_ref, gk_ref, g_ref, a_ref, s_ref):
    c = pl.program_id(2)
    C, SUB = GLA_C, GLA_SUB
    nsub = C // SUB

    @pl.when(c == 0)
    def _():
        s_ref[...] = jnp.zeros_like(s_ref)

    q = q_ref[...] * (GLA_DK ** -0.5)
    k = k_ref[...]
    v = v_ref[...].astype(BF16)
    gk = gk_ref[...]
    S = s_ref[0, 0]

    cum = _cumsum_mxu(gk) * LOG2E
    excl = cum - gk * LOG2E
    last = cum[C - 1:C, :]

    o = _dot((q * jnp.exp2(cum)).astype(BF16), S.astype(BF16))

    row_blocks = [jnp.zeros((SUB, C), F32)]
    for i in range(1, nsub):
        sl = slice(i * SUB, (i + 1) * SUB)
        n_k = i * SUB
        b_i = excl[n_k:n_k + 1, :]
        qt = (q[sl] * jnp.exp2(cum[sl] - b_i)).astype(BF16)
        kh = (k[:n_k] * jnp.exp2(b_i - cum[:n_k])).astype(BF16)
        kh = jnp.concatenate([kh, jnp.zeros((C - n_k, GLA_DK), BF16)], axis=0)
        row_blocks.append(_dot_nt(qt, kh))
    att = jnp.concatenate(row_blocks, axis=0)

    parts = []
    for d in range(SUB):
        k_d = k if d == 0 else pltpu.roll(k, d, 0)
        cum_d = cum if d == 0 else pltpu.roll(cum, d, 0)
        parts.append((q * k_d * jnp.exp2(cum - cum_d)).astype(BF16))
    ones = jnp.ones((GLA_DK, 128), BF16)
    band = _dot(jnp.concatenate(parts, axis=0), ones)
    r = lax.broadcasted_iota(jnp.int32, (C, C), 0)
    cc = lax.broadcasted_iota(jnp.int32, (C, C), 1)
    off = jnp.where((r & -SUB) == (cc & -SUB), r - cc, -1)
    for d in range(SUB):
        att = jnp.where(off == d, band[d * C:(d + 1) * C], att)

    o = o + _dot(att.astype(BF16), v)

    kd = (k * jnp.exp2(last - cum)).astype(BF16)
    dec = _col_from_row(jnp.exp2(last))
    dec = jnp.concatenate([dec] * (GLA_DV // 128), axis=1)
    s_ref[0, 0] = dec * S + _dot(kd.T, v)

    a_ref[...] = _head_rmsnorm_gate(o, g_ref[...], og_ref[...]).astype(BF16)


def gla_prompt(proj, gk, onorm_g):
    nc = SEQ // GLA_C

    def rows(b, h, c):
        return b * nc + c

    return pl.pallas_call(
        _gla_prompt_kernel,
        grid=(BATCH, GLA_H, nc),
        in_specs=[
            pl.BlockSpec((GLA_C, GLA_DK), lambda b, h, c: (rows(b, h, c), h)),
            pl.BlockSpec((GLA_C, GLA_DK), lambda b, h, c: (rows(b, h, c), GLA_H + h)),
            pl.BlockSpec((GLA_C, GLA_DV), lambda b, h, c: (rows(b, h, c), GLA_H + h)),
            pl.BlockSpec((GLA_C, GLA_DV), lambda b, h, c: (rows(b, h, c), 2 * GLA_H + h)),
            pl.BlockSpec((GLA_C, GLA_DK), lambda b, h, c: (rows(b, h, c), h)),
            pl.BlockSpec((1, GLA_DV), lambda b, h, c: (0, 0)),
        ],
        out_specs=[
            pl.BlockSpec((GLA_C, GLA_DV), lambda b, h, c: (rows(b, h, c), h)),
            pl.BlockSpec((1, 1, GLA_DK, GLA_DV), lambda b, h, c: (b, h, 0, 0)),
        ],
        out_shape=[
            jax.ShapeDtypeStruct((NP_TOK, GLA_HV), BF16),
            jax.ShapeDtypeStruct((BATCH, GLA_H, GLA_DK, GLA_DV), F32),
        ],
        compiler_params=_params(("arbitrary", "arbitrary", "arbitrary")),
        name="gla_prompt",
    )(proj, proj, proj, proj, gk, onorm_g.reshape(1, GLA_DV))


def _gla_sample_kernel(qk_ref, v_ref, og_ref, gk_ref, g_ref, s0_ref, *rest, aliased):
    a_ref, s_ref = rest[1:] if aliased else rest
    T = DEC_SEQ
    row = lax.broadcasted_iota(jnp.int32, (T, GLA_DV), 0)
    outs = []
    for h in range(GLA_H):
        ksl = slice(h * GLA_DK, (h + 1) * GLA_DK)
        vsl = slice(h * GLA_DV, (h + 1) * GLA_DV)
        q = qk_ref[:, ksl] * (GLA_DK ** -0.5)
        k = qk_ref[:, GLA_HK + h * GLA_DK:GLA_HK + (h + 1) * GLA_DK]
        v = v_ref[:, vsl]
        gk = gk_ref[:, ksl]
        S = s0_ref[0, 0, h]

        cum = _cumsum_rows(gk)
        last = cum[T - 1:T, :]
        o = _dot((q * jnp.exp(cum)).astype(BF16), S.astype(BF16))
        for d in range(T):
            k_d = k if d == 0 else pltpu.roll(k, d, 0)
            cum_d = cum if d == 0 else pltpu.roll(cum, d, 0)
            v_d = v if d == 0 else pltpu.roll(v, d, 0)
            w = jnp.sum(q * k_d * jnp.exp(jnp.minimum(cum - cum_d, 0.0)), axis=-1, keepdims=True)
            o = o + jnp.where(row >= d, w * v_d, 0.0)

        kd = k * jnp.exp(last - cum)
        stacked = jnp.concatenate(
            [kd, jnp.broadcast_to(jnp.exp(last), (T, GLA_DK)), jnp.zeros((128 - 2 * T, GLA_DK), F32)], axis=0)
        st = stacked.T
        vpad = jnp.concatenate([v, jnp.zeros((128 - T, GLA_DV), F32)], axis=0)
        lane = lax.broadcasted_iota(jnp.int32, (GLA_DK, 128), 1)
        kdt = jnp.where(lane < T, st, 0.0).astype(BF16)
        s_ref[0, 0, h] = st[:, T:T + 1] * S + _dot(kdt, vpad.astype(BF16))

        outs.append(_head_rmsnorm_gate(o, g_ref[...], og_ref[:, vsl]))
    a_ref[...] = jnp.concatenate(outs, axis=1).astype(BF16)


def gla_sample(proj, gk, onorm_g, state_all, layer, states_out=None):
    r0 = NP_TOK // DEC_SEQ
    n_a = state_all.shape[0]
    st_block = (1, 1, GLA_H, GLA_DK, GLA_DV)
    in_specs = [
        pl.BlockSpec((DEC_SEQ, 2 * GLA_HK), lambda b: (r0 + b, 0)),
        pl.BlockSpec((DEC_SEQ, GLA_HV), lambda b: (r0 + b, 1)),
        pl.BlockSpec((DEC_SEQ, GLA_HV), lambda b: (r0 + b, 2)),
        pl.BlockSpec((DEC_SEQ, GLA_HK), lambda b: (r0 + b, 0)),
        pl.BlockSpec((1, GLA_DV), lambda b: (0, 0)),
        pl.BlockSpec(st_block, lambda b: (layer, b, 0, 0, 0)),
    ]
    args = [proj, proj, proj, gk, onorm_g.reshape(1, GLA_DV), state_all]
    aliases = {}
    if states_out is not None:
        in_specs.append(pl.BlockSpec(memory_space=pl.ANY))
        args.append(states_out)
        aliases = {len(args) - 1: 1}
    return pl.pallas_call(
        functools.partial(_gla_sample_kernel, aliased=states_out is not None),
        grid=(DEC_BATCH,),
        in_specs=in_specs,
        out_specs=[
            pl.BlockSpec((DEC_SEQ, GLA_HV), lambda b: (b, 0)),
            pl.BlockSpec(st_block, lambda b: (layer, b, 0, 0, 0)),
        ],
        out_shape=[
            jax.ShapeDtypeStruct((NS_TOK, GLA_HV), BF16),
            jax.ShapeDtypeStruct((n_a, DEC_BATCH, GLA_H, GLA_DK, GLA_DV), F32),
        ],
        input_output_aliases=aliases,
        compiler_params=_params(("arbitrary",)),
        name="gla_sample",
    )(*args)


def _rope_partner(x):
    lane = lax.broadcasted_iota(jnp.int32, x.shape, 1)
    return jnp.where((lane & (MLA_ROPE - 1)) < MLA_ROPE // 2, pltpu.roll(x, x.shape[1] - MLA_ROPE // 2, 1),
                     pltpu.roll(x, MLA_ROPE // 2, 1))


def _kvprep_kernel(ckv_ref, kr_ref, g_ref, ckvn_ref, krr_ref, c4_ref, s4_ref, *, np_tiles):
    i = pl.program_id(0)
    x = ckv_ref[...]
    ms = jnp.mean(x * x, axis=-1, keepdims=True)
    ckvn_ref[...] = x * lax.rsqrt(ms + EPS) * g_ref[...]

    r = i * TM + lax.broadcasted_iota(jnp.int32, (TM, 128), 0)
    pos = jnp.where(i < np_tiles, r & (SEQ - 1), PAST + (r & (DEC_SEQ - 1))).astype(F32)
    k2 = (lax.broadcasted_iota(jnp.int32, (8, 128), 1) & (MLA_ROPE // 2 - 1)) * 2
    inv = jnp.power(jnp.float32(ROPE_BASE), -k2.astype(F32) / MLA_ROPE)[0:1, :]
    ang = pos * inv
    lane = lax.broadcasted_iota(jnp.int32, (TM, 128), 1)
    c4 = jnp.cos(ang)
    s4 = jnp.where((lane & (MLA_ROPE - 1)) < MLA_ROPE // 2, -jnp.sin(ang), jnp.sin(ang))
    c4_ref[...] = c4
    s4_ref[...] = s4
    kr = kr_ref[...]
    krr_ref[...] = (kr * c4 + _rope_partner(kr) * s4)[:, :MLA_ROPE]


def mla_kvprep(proj, kv_norm_g):
    np_tiles = NP_TOK // TM
    return pl.pallas_call(
        functools.partial(_kvprep_kernel, np_tiles=np_tiles),
        grid=(N_TOK // TM,),
        in_specs=[
            pl.BlockSpec((TM, MLA_RANK), lambda i: (i, 3072 // MLA_RANK)),
            pl.BlockSpec((TM, 128), lambda i: (i, 5632 // 128)),
            pl.BlockSpec((1, MLA_RANK), lambda i: (0, 0)),
        ],
        out_specs=[
            pl.BlockSpec((TM, MLA_RANK), lambda i: (i, 0)),
            pl.BlockSpec((TM, MLA_ROPE), lambda i: (i, 0)),
            pl.BlockSpec((TM, 128), lambda i: (i, 0)),
            pl.BlockSpec((TM, 128), lambda i: (i, 0)),
        ],
        out_shape=[
            jax.ShapeDtypeStruct((N_TOK, MLA_RANK), F32),
            jax.ShapeDtypeStruct((N_TOK, MLA_ROPE), F32),
            jax.ShapeDtypeStruct((N_TOK, 128), F32),
            jax.ShapeDtypeStruct((N_TOK, 128), F32),
        ],
        compiler_params=_params(("arbitrary",)),
        name="mla_kvprep",
    )(proj, proj, kv_norm_g.reshape(1, MLA_RANK))


def _kvup_kernel(ckv_ref, kr_ref, wuk_ref, wuv_ref, k_ref, v_ref):
    c = ckv_ref[...].astype(BF16)
    kn = _dot(c, wuk_ref[...]).astype(BF16)
    vv = _dot(c, wuv_ref[...]).astype(BF16)
    kr = kr_ref[...].astype(BF16)
    for h in range(MLA_H):
        k_ref[h, :, :MLA_NOPE] = kn[:, h * MLA_NOPE:(h + 1) * MLA_NOPE]
        k_ref[h, :, MLA_NOPE:] = kr
        v_ref[h] = vv[:, h * MLA_VH:(h + 1) * MLA_VH]


def mla_kvup(ckv_n, kr_r, w_uk2, w_uv2):
    return pl.pallas_call(
        _kvup_kernel,
        grid=(NP_TOK // TM,),
        in_specs=[
            pl.BlockSpec((TM, MLA_RANK), lambda i: (i, 0)),
            pl.BlockSpec((TM, MLA_ROPE), lambda i: (i, 0)),
            pl.BlockSpec((MLA_RANK, MLA_H * MLA_NOPE), lambda i: (0, 0)),
            pl.BlockSpec((MLA_RANK, MLA_H * MLA_VH), lambda i: (0, 0)),
        ],
        out_specs=[
            pl.BlockSpec((MLA_H, TM, MLA_QK), lambda i: (0, i, 0)),
            pl.BlockSpec((MLA_H, TM, MLA_VH), lambda i: (0, i, 0)),
        ],
        out_shape=[
            jax.ShapeDtypeStruct((MLA_H, NP_TOK, MLA_QK), BF16),
            jax.ShapeDtypeStruct((MLA_H, NP_TOK, MLA_VH), BF16),
        ],
        compiler_params=_params(("arbitrary",)),
        name="mla_kvup",
    )(ckv_n, kr_r, w_uk2, w_uv2)


FLASH_T = 512


def _rope_pair_select(x2, h):
    return jnp.where(h % 2 == 1, pltpu.roll(x2, MLA_ROPE, 1), x2)[:, :MLA_ROPE]


FLASH_G = 4


def _lanes(x, n):
    return x if n == 128 else jnp.concatenate([x] * (n // 128), axis=1)


def _flash_kernel(qt_ref, kt_ref, qn_ref, qr_ref, c4_ref, s4_ref, k_ref, v_ref, gate_ref, o_ref,
                  q_sc, m_sc, l_sc, acc_sc):
    s_id = pl.program_id(2)
    qi = qt_ref[s_id]
    ki = kt_ref[s_id]
    T, G = FLASH_T, FLASH_G

    @pl.when(ki == 0)
    def _():
        x = qr_ref[...]
        c = _lanes(c4_ref[...], G * MLA_ROPE)
        s = _lanes(s4_ref[...], G * MLA_ROPE)
        rot = (x * c + _rope_partner(x) * s) * MLA_SCALE
        for g in range(G):
            q_sc[g, :, :MLA_NOPE] = (qn_ref[:, g * MLA_NOPE:(g + 1) * MLA_NOPE] * MLA_SCALE).astype(BF16)
            q_sc[g, :, MLA_NOPE:] = rot[:, g * MLA_ROPE:(g + 1) * MLA_ROPE].astype(BF16)
        m_sc[...] = jnp.full_like(m_sc, _NEG)
        l_sc[...] = jnp.zeros_like(l_sc)
        acc_sc[...] = jnp.zeros_like(acc_sc)

    def step(masked):
        for g in range(G):
            s = _dot_nt(q_sc[g], k_ref[g])
            if masked:
                r = lax.broadcasted_iota(jnp.int32, (T, T), 0)
                c = lax.broadcasted_iota(jnp.int32, (T, T), 1)
                s = jnp.where(c <= r, s, _NEG)
            m_prev = m_sc[g]
            m_new = jnp.maximum(m_prev, jnp.max(s, axis=-1, keepdims=True))
            alpha = jnp.exp(m_prev - m_new)
            p = jnp.exp(s - _lanes(m_new, T))
            l_sc[g] = alpha * l_sc[g] + jnp.sum(p, axis=-1, keepdims=True)
            acc_sc[g] = alpha * acc_sc[g] + _dot(p.astype(BF16), v_ref[g])
            m_sc[g] = m_new

    @pl.when(ki < qi)
    def _():
        step(False)

    @pl.when(ki == qi)
    def _():
        step(True)
        for g in range(G):
            sl = slice(g * MLA_VH, (g + 1) * MLA_VH)
            o = acc_sc[g] / l_sc[g]
            o_ref[:, sl] = (o * _silu(gate_ref[:, sl])).astype(BF16)


def mla_flash(proj, c4, s4, kcat, vv):
    T, G = FLASH_T, FLASH_G
    nq = SEQ // T
    pairs = [(qi, ki) for qi in range(nq) for ki in range(qi + 1)]
    qt = jnp.asarray(np.array([p[0] for p in pairs], np.int32))
    kt = jnp.asarray(np.array([p[1] for p in pairs], np.int32))

    def qrow(b, g, s, qt, kt):
        return b * nq + qt[s]

    def krow(b, g, s, qt, kt):
        return b * nq + kt[s]

    grid_spec = pltpu.PrefetchScalarGridSpec(
        num_scalar_prefetch=2,
        grid=(BATCH, MLA_H // G, len(pairs)),
        in_specs=[
            pl.BlockSpec((T, G * MLA_NOPE), lambda b, g, s, qt, kt: (qrow(b, g, s, qt, kt), g)),
            pl.BlockSpec((T, G * MLA_ROPE),
                         lambda b, g, s, qt, kt: (qrow(b, g, s, qt, kt), 2048 // (G * MLA_ROPE) + g)),
            pl.BlockSpec((T, 128), lambda b, g, s, qt, kt: (qrow(b, g, s, qt, kt), 0)),
            pl.BlockSpec((T, 128), lambda b, g, s, qt, kt: (qrow(b, g, s, qt, kt), 0)),
            pl.BlockSpec((G, T, MLA_QK), lambda b, g, s, qt, kt: (g, krow(b, g, s, qt, kt), 0)),
            pl.BlockSpec((G, T, MLA_VH), lambda b, g, s, qt, kt: (g, krow(b, g, s, qt, kt), 0)),
            pl.BlockSpec((T, G * MLA_VH),
                         lambda b, g, s, qt, kt: (qrow(b, g, s, qt, kt), 3584 // (G * MLA_VH) + g)),
        ],
        out_specs=pl.BlockSpec((T, G * MLA_VH), lambda b, g, s, qt, kt: (qrow(b, g, s, qt, kt), g)),
        scratch_shapes=[
            pltpu.VMEM((G, T, MLA_QK), BF16),
            pltpu.VMEM((G, T, 128), F32),
            pltpu.VMEM((G, T, 128), F32),
            pltpu.VMEM((G, T, MLA_VH), F32),
        ],
    )
    return pl.pallas_call(
        _flash_kernel,
        grid_spec=grid_spec,
        out_shape=jax.ShapeDtypeStruct((NP_TOK, MLA_H * MLA_VH), BF16),
        compiler_params=_params(("arbitrary", "arbitrary", "arbitrary")),
        name="mla_flash",
    )(qt, kt, proj, proj, c4, s4, kcat, vv, proj)


def _qabs_kernel(qn_ref, qr_ref, c4_ref, s4_ref, wukt_ref, o_ref):
    h = pl.program_id(0)
    ql = _dot(qn_ref[...].astype(BF16), wukt_ref[0]) * MLA_SCALE
    x2 = qr_ref[...]
    rot = x2 * c4_ref[...] + _rope_partner(x2) * s4_ref[...]
    qr = _rope_pair_select(rot, h) * MLA_SCALE
    o_ref[:, 0, :, :MLA_RANK] = ql.reshape(DEC_BATCH, DEC_SEQ, MLA_RANK)
    o_ref[:, 0, :, MLA_RANK:] = qr.reshape(DEC_BATCH, DEC_SEQ, MLA_ROPE)


def mla_qabs(proj, c4, s4, w_ukt3):
    rb = NP_TOK // NS_TOK
    return pl.pallas_call(
        _qabs_kernel,
        grid=(MLA_H,),
        in_specs=[
            pl.BlockSpec((NS_TOK, 128), lambda h: (rb, h)),
            pl.BlockSpec((NS_TOK, 128), lambda h: (rb, 2048 // 128 + h // 2)),
            pl.BlockSpec((NS_TOK, 128), lambda h: (rb, 0)),
            pl.BlockSpec((NS_TOK, 128), lambda h: (rb, 0)),
            pl.BlockSpec((1, MLA_NOPE, MLA_RANK), lambda h: (h, 0, 0)),
        ],
        out_specs=pl.BlockSpec((DEC_BATCH, 1, DEC_SEQ, MLA_LAT), lambda h: (0, h, 0, 0)),
        out_shape=jax.ShapeDtypeStruct((DEC_BATCH, MLA_H, DEC_SEQ, MLA_LAT), F32),
        compiler_params=_params(("arbitrary",)),
        name="mla_qabs",
    )(proj, proj, c4, s4, w_ukt3)


DEC_PAGES = 16
DEC_STEPS = N_PAGES // DEC_PAGES


def _decode_kernel(pt_ref, q_ref, *rest):
    ckv_refs = rest[:DEC_PAGES]
    kr_refs = rest[DEC_PAGES:2 * DEC_PAGES]
    cnew_ref, knew_ref, o_ref, m_sc, l_sc, acc_sc = rest[2 * DEC_PAGES:]
    s_id = pl.program_id(1)
    R = MLA_H * DEC_SEQ

    @pl.when(s_id == 0)
    def _():
        m_sc[...] = jnp.full_like(m_sc, _NEG)
        l_sc[...] = jnp.zeros_like(l_sc)
        acc_sc[...] = jnp.zeros_like(acc_sc)

    q = q_ref[0].reshape(R, MLA_LAT)
    ql = q[:, :MLA_RANK].astype(BF16)
    qr = q[:, MLA_RANK:].astype(BF16)

    def online(s, vals):
        m_prev = m_sc[...]
        m_new = jnp.maximum(m_prev, jnp.max(s, axis=-1, keepdims=True))
        alpha = jnp.exp(m_prev - m_new)
        p = jnp.exp(s - _lanes(m_new, s.shape[1]))
        l_sc[...] = alpha * l_sc[...] + jnp.sum(p, axis=-1, keepdims=True)
        acc_sc[...] = _lanes(alpha, MLA_RANK) * acc_sc[...] + _dot(p.astype(BF16), vals)
        m_sc[...] = m_new

    @pl.when(s_id < DEC_STEPS)
    def _():
        kv = jnp.concatenate([ckv_refs[r][0, 0].astype(BF16) for r in range(DEC_PAGES)], axis=0)
        krt = jnp.concatenate([kr_refs[r][0, 0].astype(BF16) for r in range(DEC_PAGES)], axis=1)
        online(_dot_nt(ql, kv) + _dot(qr, krt), kv)

    @pl.when(s_id == DEC_STEPS)
    def _():
        cn = jnp.concatenate([cnew_ref[...], jnp.zeros((PAGE - DEC_SEQ, MLA_RANK), F32)], axis=0).astype(BF16)
        kn = jnp.concatenate([knew_ref[...], jnp.zeros((PAGE - DEC_SEQ, MLA_ROPE), F32)], axis=0).astype(BF16)
        s = _dot_nt(ql, cn) + _dot_nt(qr, kn)
        t = lax.broadcasted_iota(jnp.int32, (R, PAGE), 0) & (DEC_SEQ - 1)
        j = lax.broadcasted_iota(jnp.int32, (R, PAGE), 1)
        online(jnp.where(j <= t, s, _NEG), cn)
        o = acc_sc[...] / _lanes(l_sc[...], MLA_RANK)
        o_ref[0] = o.reshape(MLA_H, DEC_SEQ, MLA_RANK)


def mla_decode(qcat, cache_ckv, cache_krt, layer, page_table, ckv_n, kr_r):
    r0 = NP_TOK // DEC_SEQ

    def page_idx(r):
        return lambda b, s, pt: (layer, pt[b * N_PAGES + jnp.minimum(s, DEC_STEPS - 1) * DEC_PAGES + r], 0, 0)

    in_specs = [pl.BlockSpec((1, MLA_H, DEC_SEQ, MLA_LAT), lambda b, s, pt: (b, 0, 0, 0))]
    in_specs += [pl.BlockSpec((1, 1, PAGE, MLA_RANK), page_idx(r)) for r in range(DEC_PAGES)]
    in_specs += [pl.BlockSpec((1, 1, MLA_ROPE, PAGE), page_idx(r)) for r in range(DEC_PAGES)]
    in_specs += [
        pl.BlockSpec((DEC_SEQ, MLA_RANK), lambda b, s, pt: (r0 + b, 0)),
        pl.BlockSpec((DEC_SEQ, MLA_ROPE), lambda b, s, pt: (r0 + b, 0)),
    ]

    grid_spec = pltpu.PrefetchScalarGridSpec(
        num_scalar_prefetch=1,
        grid=(DEC_BATCH, DEC_STEPS + 1),
        in_specs=in_specs,
        out_specs=pl.BlockSpec((1, MLA_H, DEC_SEQ, MLA_RANK), lambda b, s, pt: (b, 0, 0, 0)),
        scratch_shapes=[
            pltpu.VMEM((MLA_H * DEC_SEQ, 128), F32),
            pltpu.VMEM((MLA_H * DEC_SEQ, 128), F32),
            pltpu.VMEM((MLA_H * DEC_SEQ, MLA_RANK), F32),
        ],
    )
    args = [page_table.reshape(-1), qcat] + [cache_ckv] * DEC_PAGES + [cache_krt] * DEC_PAGES + [ckv_n, kr_r]
    return pl.pallas_call(
        _decode_kernel,
        grid_spec=grid_spec,
        out_shape=jax.ShapeDtypeStruct((DEC_BATCH, MLA_H, DEC_SEQ, MLA_RANK), F32),
        compiler_params=_params(("arbitrary", "arbitrary")),
        name="mla_decode",
    )(*args)


def _uvup_kernel(ol_ref, wuv_ref, gate_ref, a_ref):
    ol = ol_ref[...].reshape(NS_TOK, MLA_RANK).astype(BF16)
    a_ref[...] = (_dot(ol, wuv_ref[0]) * _silu(gate_ref[...])).astype(BF16)


def mla_uvup(o_lat, w_uv3, proj):
    rb = NP_TOK // NS_TOK
    return pl.pallas_call(
        _uvup_kernel,
        grid=(MLA_H,),
        in_specs=[
            pl.BlockSpec((DEC_BATCH, 1, DEC_SEQ, MLA_RANK), lambda h: (0, h, 0, 0)),
            pl.BlockSpec((1, MLA_RANK, MLA_VH), lambda h: (h, 0, 0)),
            pl.BlockSpec((NS_TOK, 128), lambda h: (rb, 3584 // 128 + h)),
        ],
        out_specs=pl.BlockSpec((NS_TOK, MLA_VH), lambda h: (0, h)),
        out_shape=jax.ShapeDtypeStruct((NS_TOK, MLA_H * MLA_VH), BF16),
        compiler_params=_params(("arbitrary",)),
        name="mla_uvup",
    )(o_lat, w_uv3, proj)


POOL_HALO = 16


def _pool_prompt_kernel(u_ref, halo_ref, gate_ref, wg_ref, sc_ref, a_ref, ext_sc, *, tiles_per_seq):
    i = pl.program_id(0)
    g = pl.program_id(1)
    first = (i % tiles_per_seq) == 0
    u = u_ref[...]
    ext_sc[:POOL_HALO, :] = jnp.where(first, 0.0, halo_ref[...])
    ext_sc[POOL_HALO:, :] = u
    t = (i % tiles_per_seq) * TM + lax.broadcasted_iota(jnp.int32, (TM, 1), 0)

    for gi, w in enumerate(POOL_WINDOWS):
        @pl.when(g == gi)
        def _(w=w):
            acc = u
            for j in range(1, w):
                acc = acc + ext_sc[POOL_HALO - j:POOL_HALO - j + TM, :]
            cnt = jnp.minimum(t + 1, w).astype(F32)
            p = (acc / cnt - u).astype(BF16)
            z = _dot(p, wg_ref[0]) * sc_ref[...]
            a_ref[...] = (z * _silu(gate_ref[...])).astype(BF16)


def pool_prompt(proj, w_grp, pscale):
    tiles_per_seq = SEQ // TM
    hb = TM // POOL_HALO
    return pl.pallas_call(
        functools.partial(_pool_prompt_kernel, tiles_per_seq=tiles_per_seq),
        grid=(NP_TOK // TM, len(POOL_WINDOWS)),
        in_specs=[
            pl.BlockSpec((TM, POOL_G), lambda i, g: (i, g)),
            pl.BlockSpec((POOL_HALO, POOL_G), lambda i, g: (jnp.maximum(i * hb - 1, 0), g)),
            pl.BlockSpec((TM, POOL_G), lambda i, g: (i, len(POOL_WINDOWS) + g)),
            pl.BlockSpec((1, POOL_G, POOL_G), lambda i, g: (g, 0, 0)),
            pl.BlockSpec((1, POOL_G), lambda i, g: (0, g)),
        ],
        out_specs=pl.BlockSpec((TM, POOL_G), lambda i, g: (i, g)),
        out_shape=jax.ShapeDtypeStruct((NP_TOK, D), BF16),
        scratch_shapes=[pltpu.VMEM((POOL_HALO + TM, POOL_G), F32)],
        compiler_params=_params(("arbitrary", "arbitrary")),
        name="pool_prompt",
    )(proj, proj, proj, w_grp, pscale)


def _pool_sample_kernel(u_ref, gate_ref, hist_ref, wg_ref, sc_ref, a_ref):
    g = pl.program_id(0)

    def seq(r):
        return hist_ref[r] if r < POOL_HIST else u_ref[r - POOL_HIST]

    for gi, w in enumerate(POOL_WINDOWS):
        @pl.when(g == gi)
        def _(w=w):
            ps = []
            for t in range(DEC_SEQ):
                acc = seq(POOL_HIST + t)
                for j in range(1, w):
                    acc = acc + seq(POOL_HIST + t - j)
                ps.append(acc / float(w) - u_ref[t])
            p = jnp.concatenate(ps, axis=0).astype(BF16)
            z = _dot(p, wg_ref[0]) * sc_ref[...]
            gate = gate_ref[...].reshape(NS_TOK, POOL_G)
            a_ref[...] = (z * _silu(gate)).astype(BF16).reshape(DEC_SEQ, DEC_BATCH, POOL_G)


def pool_sample(proj_t, hist_t, w_grp, pscale):
    ng = len(POOL_WINDOWS)
    return pl.pallas_call(
        _pool_sample_kernel,
        grid=(ng,),
        in_specs=[
            pl.BlockSpec((DEC_SEQ, DEC_BATCH, POOL_G), lambda g: (0, 0, g)),
            pl.BlockSpec((DEC_SEQ, DEC_BATCH, POOL_G), lambda g: (0, 0, ng + g)),
            pl.BlockSpec((POOL_HIST, DEC_BATCH, POOL_G), lambda g: (0, 0, g)),
            pl.BlockSpec((1, POOL_G, POOL_G), lambda g: (g, 0, 0)),
            pl.BlockSpec((1, POOL_G), lambda g: (0, g)),
        ],
        out_specs=pl.BlockSpec((DEC_SEQ, DEC_BATCH, POOL_G), lambda g: (0, 0, g)),
        out_shape=jax.ShapeDtypeStruct((DEC_SEQ, DEC_BATCH, D), BF16),
        compiler_params=_params(("arbitrary",)),
        name="pool_sample",
    )(proj_t, proj_t, hist_t, w_grp, pscale)


def kernel(x_prompt, x_sample, c_prompt, c_sample, state_gla, cache_ckv, cache_kr, state_pool, page_table, norm_g, ada_w, ada_b, final_norm_g, gla_w_in, gla_w_gate_up, gla_b_gate, gla_onorm_g, gla_w_out, mla_w_in, mla_kv_norm_g, mla_w_uk, mla_w_uv, mla_w_out, pool_w_in, pool_w_grp, pool_scale, pool_w_out):
    x_all = jnp.concatenate([x_prompt.reshape(NP_TOK, D), x_sample.reshape(NS_TOK, D)], axis=0)

    n_c = BATCH + DEC_BATCH
    c_all = jnp.concatenate([c_prompt, c_sample, jnp.zeros((8 - n_c % 8, D), F32)], axis=0)
    mod = ada_mod(c_all, ada_w, ada_b)

    gla_states_p, gla_states_s = [], None
    ckv_rows, kr_rows, pool_p, pool_s = [], [], [], []
    ia = ib = ic = 0
    y_all = None
    for l in range(DEPTH):
        mod_p = mod[l, :BATCH].reshape(BATCH, 1, 3 * D)
        mod_s = jnp.repeat(mod[l, BATCH:n_c], DEC_SEQ, axis=0)
        mixer = LAYER_MIXER[l]
        if mixer == 0:
            w_in = gla_w_in[ia]
            cut = 2 * GLA_HK + GLA_HV
            w_main = jnp.concatenate([w_in[:, :cut], w_in[:, cut + GLA_RANK:]], axis=1).astype(BF16)
            w_glr = jnp.pad(w_in[:, cut:cut + GLA_RANK], ((0, 0), (0, 128 - GLA_RANK))).astype(BF16)
            w_up = jnp.pad(gla_w_gate_up[ia], ((0, 128 - GLA_RANK), (0, 0))).astype(BF16)
            proj, gk = norm_proj(x_all, norm_g[l], mod_p, mod_s, w_main, 1024,
                                 gk_weights=(w_glr, w_up, gla_b_gate[ia].reshape(1, GLA_HK)))
            a_p, st_p = gla_prompt(proj, gk, gla_onorm_g[ia])
            a_s, gla_states_s = gla_sample(proj, gk, gla_onorm_g[ia], state_gla, ia, states_out=gla_states_s)
            gla_states_p.append(st_p)
            w_out = gla_w_out[ia].astype(BF16)
            ia += 1
        elif mixer == 1:
            w_in = mla_w_in[ib]
            nq = MLA_H * MLA_QK
            wq = w_in[:, :nq].reshape(D, MLA_H, MLA_QK)
            w_perm = jnp.concatenate([
                wq[:, :, :MLA_NOPE].reshape(D, MLA_H * MLA_NOPE),
                wq[:, :, MLA_NOPE:].reshape(D, MLA_H * MLA_ROPE),
                w_in[:, nq:nq + MLA_RANK],
                w_in[:, nq + MLA_RANK + MLA_ROPE:],
                w_in[:, nq + MLA_RANK:nq + MLA_RANK + MLA_ROPE],
                jnp.zeros((D, 128 - MLA_ROPE), F32),
            ], axis=1).astype(BF16)
            proj = norm_proj(x_all, norm_g[l], mod_p, mod_s, w_perm, 1152)
            ckv_n, kr_r, c4, s4 = mla_kvprep(proj, mla_kv_norm_g[ib])
            w_uk2 = mla_w_uk[ib].reshape(MLA_RANK, MLA_H * MLA_NOPE).astype(BF16)
            w_uv2 = mla_w_uv[ib].reshape(MLA_RANK, MLA_H * MLA_VH).astype(BF16)
            w_ukt3 = jnp.transpose(mla_w_uk[ib], (1, 2, 0)).astype(BF16)
            w_uv3 = jnp.transpose(mla_w_uv[ib], (1, 0, 2)).astype(BF16)
            kcat, vv = mla_kvup(ckv_n, kr_r, w_uk2, w_uv2)
            a_p = mla_flash(proj, c4, s4, kcat, vv)
            qcat = mla_qabs(proj, c4, s4, w_ukt3)
            cache_krt = jnp.swapaxes(cache_kr, 2, 3)
            o_lat = mla_decode(qcat, cache_ckv, cache_krt, ib, page_table, ckv_n, kr_r)
            a_s = mla_uvup(o_lat, w_uv3, proj)
            ckv_rows.append(ckv_n)
            kr_rows.append(kr_r)
            w_out = mla_w_out[ib].astype(BF16)
            ib += 1
        else:
            proj = norm_proj(x_all, norm_g[l], mod_p, mod_s, pool_w_in[ic].astype(BF16), 1024)
            w_grp = pool_w_grp[ic].astype(BF16)
            pscale = pool_scale[ic].reshape(1, D)
            a_p = pool_prompt(proj, w_grp, pscale)
            proj_t = jnp.transpose(proj[NP_TOK:].reshape(DEC_BATCH, DEC_SEQ, 2 * D), (1, 0, 2))
            hist_t = jnp.transpose(state_pool[ic], (1, 0, 2))
            a_st = pool_sample(proj_t, hist_t, w_grp, pscale)
            a_s = jnp.transpose(a_st, (1, 0, 2)).reshape(NS_TOK, D)
            u_s = proj[NP_TOK:, :D].reshape(DEC_BATCH, DEC_SEQ, D)
            pool_p.append(proj[:NP_TOK].reshape(BATCH, SEQ, 2 * D)[:, SEQ - POOL_HIST:, :D])
            pool_s.append(jnp.concatenate([state_pool[ic][:, DEC_SEQ:, :], u_s], axis=1))
            w_out = pool_w_out[ic].astype(BF16)
            ic += 1
        if l == DEPTH - 1:
            y_p, y_s = out_proj(a_p, a_s, w_out, x_all, mod_p, mod_s, final_g=final_norm_g)
        else:
            x_all = out_proj(a_p, a_s, w_out, x_all, mod_p, mod_s)

    y_prompt = y_p.reshape(BATCH, SEQ, D)
    y_sample = y_s.reshape(DEC_BATCH, DEC_SEQ, D)
    ckv_all = jnp.stack(ckv_rows)
    kr_all = jnp.stack(kr_rows)
    return (
        y_prompt,
        y_sample,
        jnp.stack(gla_states_p),
        gla_states_s,
        ckv_all[:, :NP_TOK].reshape(-1, BATCH, SEQ, MLA_RANK),
        kr_all[:, :NP_TOK].reshape(-1, BATCH, SEQ, MLA_ROPE),
        ckv_all[:, NP_TOK:].reshape(-1, DEC_BATCH, DEC_SEQ, MLA_RANK),
        kr_all[:, NP_TOK:].reshape(-1, DEC_BATCH, DEC_SEQ, MLA_ROPE),
        jnp.stack(pool_p),
        jnp.stack(pool_s),
    )
```

```python
import functools

import numpy as np
import jax
import jax.numpy as jnp
from jax import lax
from jax.experimental import pallas as pl
from jax.experimental.pallas import tpu as pltpu

F32 = jnp.float32
BF16 = jnp.bfloat16

D = 2048
BATCH = 4
SEQ = 2048
DEC_BATCH = 128
DEC_SEQ = 8
PAGE = 128
N_PAGES = 64
PAST = N_PAGES * PAGE
DEPTH = 4
LAYER_MIXER = (0, 1, 2, 0)
EPS = 1e-6

NP_TOK = BATCH * SEQ
NS_TOK = DEC_BATCH * DEC_SEQ
N_TOK = NP_TOK + NS_TOK

GLA_H = 4
GLA_DK = 256
GLA_DV = 512
GLA_RANK = 16
GLA_TAU = 16.0
GLA_HK = GLA_H * GLA_DK
GLA_HV = GLA_H * GLA_DV
GLA_C = 128
GLA_SUB = 16

MLA_H = 16
MLA_NOPE = 128
MLA_ROPE = 64
MLA_VH = 128
MLA_RANK = 512
MLA_SCALE = (MLA_NOPE + MLA_ROPE) ** -0.5
MLA_QK = MLA_NOPE + MLA_ROPE
MLA_LAT = MLA_RANK + MLA_ROPE
MLA_N = 5760
ROPE_BASE = 10000.0

POOL_WINDOWS = (2, 4, 8, 16)
POOL_G = 512
POOL_HIST = 15

TM = 512
TM_OUT = 256
VMEM_LIMIT = 56 * 1024 * 1024

_NEG = -1e30


def _params(sem):
    return pltpu.CompilerParams(dimension_semantics=sem, vmem_limit_bytes=VMEM_LIMIT)


def _silu(x):
    return x * (1.0 / (1.0 + jnp.exp(-x)))


def _dot(a, b):
    return jnp.dot(a, b, preferred_element_type=F32)


def _dot_nt(a, b):
    return lax.dot_general(a, b, (((1,), (1,)), ((), ())), preferred_element_type=F32)


def _ada_kernel(c_ref, w_ref, b_ref, o_ref):
    sc = _silu(c_ref[...]).astype(BF16)
    o_ref[0] = _dot(sc, w_ref[0].astype(BF16)) + b_ref[0]


def ada_mod(c_all, ada_w, ada_b):
    rows = c_all.shape[0]
    tn = 768
    return pl.pallas_call(
        _ada_kernel,
        grid=(DEPTH, 3 * D // tn),
        in_specs=[
            pl.BlockSpec((rows, D), lambda l, j: (0, 0)),
            pl.BlockSpec((1, D, tn), lambda l, j: (l, 0, j)),
            pl.BlockSpec((1, 1, tn), lambda l, j: (l, 0, j)),
        ],
        out_specs=pl.BlockSpec((1, rows, tn), lambda l, j: (l, 0, j)),
        out_shape=jax.ShapeDtypeStruct((DEPTH, rows, 3 * D), F32),
        compiler_params=_params(("arbitrary", "arbitrary")),
        name="ada_mod",
    )(c_all, ada_w, ada_b.reshape(DEPTH, 1, 3 * D))


def _mod_rows(i, np_tiles, p_ref, s_ref):
    return jnp.where(i < np_tiles, p_ref[0], s_ref[...])


def _proj_kernel(x_ref, g_ref, shp_ref, scp_ref, shs_ref, scs_ref, w_ref, *rest, np_tiles, with_gk):
    if with_gk:
        wg_ref, wu_ref, bg_ref, o_ref, gk_ref, h_sc = rest
    else:
        o_ref, h_sc = rest
    i = pl.program_id(0)
    j = pl.program_id(1)

    @pl.when(j == 0)
    def _():
        x = x_ref[...]
        ms = jnp.mean(x * x, axis=-1, keepdims=True)
        y = x * lax.rsqrt(ms + EPS) * g_ref[...]
        shift = _mod_rows(i, np_tiles, shp_ref, shs_ref)
        scale = _mod_rows(i, np_tiles, scp_ref, scs_ref)
        h = (y * (1.0 + scale) + shift).astype(BF16)
        h_sc[...] = h
        if with_gk:
            glr = _dot(h, wg_ref[...]).astype(BF16)
            z = _dot(glr, wu_ref[...]) + bg_ref[...]
            gk_ref[...] = (jnp.minimum(z, 0.0) - jnp.log1p(jnp.exp(-jnp.abs(z)))) / GLA_TAU

    o_ref[...] = _dot(h_sc[...], w_ref[...])


def norm_proj(x_all, norm_g, mod_p, mod_s, w, tn, gk_weights=None):
    n = w.shape[1]
    np_tiles = NP_TOK // TM
    tiles_per_seq = SEQ // TM
    grid = (N_TOK // TM, n // tn)

    def p_idx(col):
        return lambda i, j: (jnp.minimum(i // tiles_per_seq, BATCH - 1), 0, col)

    def s_idx(col):
        return lambda i, j: (jnp.maximum(i - np_tiles, 0), col)

    in_specs = [
        pl.BlockSpec((TM, D), lambda i, j: (i, 0)),
        pl.BlockSpec((1, D), lambda i, j: (0, 0)),
        pl.BlockSpec((1, 1, D), p_idx(0)),
        pl.BlockSpec((1, 1, D), p_idx(1)),
        pl.BlockSpec((TM, D), s_idx(0)),
        pl.BlockSpec((TM, D), s_idx(1)),
        pl.BlockSpec((D, tn), lambda i, j: (0, j)),
    ]
    args = [x_all, norm_g.reshape(1, D), mod_p, mod_p, mod_s, mod_s, w]
    out_specs = pl.BlockSpec((TM, tn), lambda i, j: (i, j))
    out_shape = jax.ShapeDtypeStruct((N_TOK, n), F32)
    with_gk = gk_weights is not None
    if with_gk:
        wg, wu, bg = gk_weights
        in_specs += [
            pl.BlockSpec(wg.shape, lambda i, j: (0, 0)),
            pl.BlockSpec(wu.shape, lambda i, j: (0, 0)),
            pl.BlockSpec(bg.shape, lambda i, j: (0, 0)),
        ]
        args += [wg, wu, bg]
        out_specs = [out_specs, pl.BlockSpec((TM, GLA_HK), lambda i, j: (i, 0))]
        out_shape = [out_shape, jax.ShapeDtypeStruct((N_TOK, GLA_HK), F32)]
    return pl.pallas_call(
        functools.partial(_proj_kernel, np_tiles=np_tiles, with_gk=with_gk),
        grid=grid,
        in_specs=in_specs,
        out_specs=out_specs,
        out_shape=out_shape,
        scratch_shapes=[pltpu.VMEM((TM, D), BF16)],
        compiler_params=_params(("arbitrary", "arbitrary")),
        name="norm_proj_gk" if with_gk else "norm_proj",
    )(*args)


def _out_kernel(ap_ref, as_ref, w_ref, x_ref, gp_ref, gs_ref, *rest, np_tiles, final):
    i = pl.program_id(0)
    a = jnp.where(i < np_tiles, ap_ref[...], as_ref[...])
    y = _dot(a, w_ref[...])
    gate = _mod_rows(i, np_tiles, gp_ref, gs_ref)
    xn = x_ref[...] + gate * y
    if not final:
        (o_ref,) = rest
        o_ref[...] = xn
        return
    fg_ref, yp_ref, ys_ref = rest
    ms = jnp.mean(xn * xn, axis=-1, keepdims=True)
    yn = xn * lax.rsqrt(ms + EPS) * fg_ref[...]

    @pl.when(i < np_tiles)
    def _():
        yp_ref[...] = yn

    @pl.when(i >= np_tiles)
    def _():
        ys_ref[...] = yn


def out_proj(a_p, a_s, w_out, x_all, mod_p, mod_s, final_g=None):
    tm = TM_OUT
    np_tiles = NP_TOK // tm
    tiles_per_seq = SEQ // tm
    final = final_g is not None
    in_specs = [
        pl.BlockSpec((tm, D), lambda i: (jnp.minimum(i, np_tiles - 1), 0)),
        pl.BlockSpec((tm, D), lambda i: (jnp.maximum(i - np_tiles, 0), 0)),
        pl.BlockSpec((D, D), lambda i: (0, 0)),
        pl.BlockSpec((tm, D), lambda i: (i, 0)),
        pl.BlockSpec((1, 1, D), lambda i: (jnp.minimum(i // tiles_per_seq, BATCH - 1), 0, 2)),
        pl.BlockSpec((tm, D), lambda i: (jnp.maximum(i - np_tiles, 0), 2)),
    ]
    args = [a_p, a_s, w_out, x_all, mod_p, mod_s]
    out_specs = pl.BlockSpec((tm, D), lambda i: (i, 0))
    out_shape = jax.ShapeDtypeStruct((N_TOK, D), F32)
    if final:
        in_specs.append(pl.BlockSpec((1, D), lambda i: (0, 0)))
        args.append(final_g.reshape(1, D))
        out_specs = [
            pl.BlockSpec((tm, D), lambda i: (jnp.minimum(i, np_tiles - 1), 0)),
            pl.BlockSpec((tm, D), lambda i: (jnp.maximum(i - np_tiles, 0), 0)),
        ]
        out_shape = [jax.ShapeDtypeStruct((NP_TOK, D), F32), jax.ShapeDtypeStruct((NS_TOK, D), F32)]
    return pl.pallas_call(
        functools.partial(_out_kernel, np_tiles=np_tiles, final=final),
        grid=(N_TOK // tm,),
        in_specs=in_specs,
        out_specs=out_specs,
        out_shape=out_shape,
        compiler_params=_params(("arbitrary",)),
        name="out_proj_final" if final else "out_proj",
    )(*args)


def _cumsum_rows(x):
    n = x.shape[0]
    row = lax.broadcasted_iota(jnp.int32, x.shape, 0)
    s = 1
    while s < n:
        x = x + jnp.where(row >= s, pltpu.roll(x, s, 0), 0.0)
        s *= 2
    return x


LOG2E = 1.4426950408889634


def _cumsum_mxu(x):
    n = x.shape[0]
    r = lax.broadcasted_iota(jnp.int32, (n, n), 0)
    c = lax.broadcasted_iota(jnp.int32, (n, n), 1)
    tri = jnp.where(c <= r, 1.0, 0.0).astype(BF16)
    hi = x.astype(BF16)
    rem = x - hi.astype(F32)
    mid = rem.astype(BF16)
    lo = (rem - mid.astype(F32)).astype(BF16)
    return _dot(tri, hi) + _dot(tri, mid) + _dot(tri, lo)


def _col_from_row(v):
    return jnp.broadcast_to(v, (128, v.shape[1])).T


def _head_rmsnorm_gate(o, g, og):
    ms = jnp.mean(o * o, axis=-1, keepdims=True)
    return (o * lax.rsqrt(ms + EPS) * g) * _silu(og)


def _gla_chunk(q, k, v, gk, S):
    C, SUB = GLA_C, GLA_SUB
    nsub = C // SUB

    cum = _cumsum_mxu(gk) * LOG2E
    excl = cum - gk * LOG2E
    last = cum[C - 1:C, :]

    o = _dot((q * jnp.exp2(cum)).astype(BF16), S.astype(BF16))

    row_blocks = [jnp.zeros((SUB, C), F32)]
    for i in range(1, nsub):
        sl = slice(i * SUB, (i + 1) * SUB)
        n_k = i * SUB
        b_i = excl[n_k:n_k + 1, :]
        qt = (q[sl] * jnp.exp2(cum[sl] - b_i)).astype(BF16)
        kh = (k[:n_k] * jnp.exp2(b_i - cum[:n_k])).astype(BF16)
        kh = jnp.concatenate([kh, jnp.zeros((C - n_k, GLA_DK), BF16)], axis=0)
        row_blocks.append(_dot_nt(qt, kh))
    att = jnp.concatenate(row_blocks, axis=0)

    parts = []
    for d in range(SUB):
        k_d = k if d == 0 else pltpu.roll(k, d, 0)
        cum_d = cum if d == 0 else pltpu.roll(cum, d, 0)
        parts.append((q * k_d * jnp.exp2(cum - cum_d)).astype(BF16))
    ones = jnp.ones((GLA_DK, 128), BF16)
    band = _dot(jnp.concatenate(parts, axis=0), ones)
    r = lax.broadcasted_iota(jnp.int32, (C, C), 0)
    cc = lax.broadcasted_iota(jnp.int32, (C, C), 1)
    off = jnp.where((r & -SUB) == (cc & -SUB), r - cc, -1)
    for d in range(SUB):
        att = jnp.where(off == d, band[d * C:(d + 1) * C], att)

    o = o + _dot(att.astype(BF16), v)

    kd = (k * jnp.exp2(last - cum)).astype(BF16)
    dec = _col_from_row(jnp.exp2(last))
    dec = jnp.concatenate([dec] * (GLA_DV // 128), axis=1)
    return o, dec * S + _dot(kd.T, v)


def _gla_prompt_kernel(qk_ref, v_ref, og_ref, gk_ref, g_ref, a_ref, s_ref):
    @pl.when(pl.program_id(1) == 0)
    def _():
        s_ref[...] = jnp.zeros_like(s_ref)

    for h in range(GLA_H):
        ksl = slice(h * GLA_DK, (h + 1) * GLA_DK)
        vsl = slice(h * GLA_DV, (h + 1) * GLA_DV)
        q = qk_ref[:, ksl] * (GLA_DK ** -0.5)
        k = qk_ref[:, GLA_HK + h * GLA_DK:GLA_HK + (h + 1) * GLA_DK]
        o, s_new = _gla_chunk(q, k, v_ref[:, vsl].astype(BF16), gk_ref[:, ksl], s_ref[0, h])
        s_ref[0, h] = s_new
        a_ref[:, vsl] = _head_rmsnorm_gate(o, g_ref[...], og_ref[:, vsl]).astype(BF16)


def gla_prompt(proj, gk, onorm_g):
    nc = SEQ // GLA_C

    def rows(b, c):
        return b * nc + c

    return pl.pallas_call(
        _gla_prompt_kernel,
        grid=(BATCH, nc),
        in_specs=[
            pl.BlockSpec((GLA_C, 2 * GLA_HK), lambda b, c: (rows(b, c), 0)),
            pl.BlockSpec((GLA_C, GLA_HV), lambda b, c: (rows(b, c), 1)),
            pl.BlockSpec((GLA_C, GLA_HV), lambda b, c: (rows(b, c), 2)),
            pl.BlockSpec((GLA_C, GLA_HK), lambda b, c: (rows(b, c), 0)),
            pl.BlockSpec((1, GLA_DV), lambda b, c: (0, 0)),
        ],
        out_specs=[
            pl.BlockSpec((GLA_C, GLA_HV), lambda b, c: (rows(b, c), 0)),
            pl.BlockSpec((1, GLA_H, GLA_DK, GLA_DV), lambda b, c: (b, 0, 0, 0)),
        ],
        out_shape=[
            jax.ShapeDtypeStruct((NP_TOK, GLA_HV), BF16),
            jax.ShapeDtypeStruct((BATCH, GLA_H, GLA_DK, GLA_DV), F32),
        ],
        compiler_params=_params(("arbitrary", "arbitrary")),
        name="gla_prompt",
    )(proj, proj, proj, gk, onorm_g.reshape(1, GLA_DV))


def _gla_sample_kernel(qk_ref, v_ref, og_ref, gk_ref, g_ref, s0_ref, *rest, aliased):
    a_ref, s_ref = rest[1:] if aliased else rest
    T = DEC_SEQ
    row = lax.broadcasted_iota(jnp.int32, (T, GLA_DV), 0)
    outs = []
    for h in range(GLA_H):
        ksl = slice(h * GLA_DK, (h + 1) * GLA_DK)
        vsl = slice(h * GLA_DV, (h + 1) * GLA_DV)
        q = qk_ref[:, ksl] * (GLA_DK ** -0.5)
        k = qk_ref[:, GLA_HK + h * GLA_DK:GLA_HK + (h + 1) * GLA_DK]
        v = v_ref[:, vsl]
        gk = gk_ref[:, ksl]
        S = s0_ref[0, 0, h]

        cum = _cumsum_rows(gk)
        last = cum[T - 1:T, :]
        o = _dot((q * jnp.exp(cum)).astype(BF16), S.astype(BF16))
        for d in range(T):
            k_d = k if d == 0 else pltpu.roll(k, d, 0)
            cum_d = cum if d == 0 else pltpu.roll(cum, d, 0)
            v_d = v if d == 0 else pltpu.roll(v, d, 0)
            w = jnp.sum(q * k_d * jnp.exp(jnp.minimum(cum - cum_d, 0.0)), axis=-1, keepdims=True)
            o = o + jnp.where(row >= d, w * v_d, 0.0)

        kd = k * jnp.exp(last - cum)
        stacked = jnp.concatenate(
            [kd, jnp.broadcast_to(jnp.exp(last), (T, GLA_DK)), jnp.zeros((128 - 2 * T, GLA_DK), F32)], axis=0)
        st = stacked.T
        vpad = jnp.concatenate([v, jnp.zeros((128 - T, GLA_DV), F32)], axis=0)
        lane = lax.broadcasted_iota(jnp.int32, (GLA_DK, 128), 1)
        kdt = jnp.where(lane < T, st, 0.0).astype(BF16)
        s_ref[0, 0, h] = st[:, T:T + 1] * S + _dot(kdt, vpad.astype(BF16))

        outs.append(_head_rmsnorm_gate(o, g_ref[...], og_ref[:, vsl]))
    a_ref[...] = jnp.concatenate(outs, axis=1).astype(BF16)


def gla_sample(proj, gk, onorm_g, state_all, layer, states_out=None):
    r0 = NP_TOK // DEC_SEQ
    n_a = state_all.shape[0]
    st_block = (1, 1, GLA_H, GLA_DK, GLA_DV)
    in_specs = [
        pl.BlockSpec((DEC_SEQ, 2 * GLA_HK), lambda b: (r0 + b, 0)),
        pl.BlockSpec((DEC_SEQ, GLA_HV), lambda b: (r0 + b, 1)),
        pl.BlockSpec((DEC_SEQ, GLA_HV), lambda b: (r0 + b, 2)),
        pl.BlockSpec((DEC_SEQ, GLA_HK), lambda b: (r0 + b, 0)),
        pl.BlockSpec((1, GLA_DV), lambda b: (0, 0)),
        pl.BlockSpec(st_block, lambda b: (layer, b, 0, 0, 0)),
    ]
    args = [proj, proj, proj, gk, onorm_g.reshape(1, GLA_DV), state_all]
    aliases = {}
    if states_out is not None:
        in_specs.append(pl.BlockSpec(memory_space=pl.ANY))
        args.append(states_out)
        aliases = {len(args) - 1: 1}
    return pl.pallas_call(
        functools.partial(_gla_sample_kernel, aliased=states_out is not None),
        grid=(DEC_BATCH,),
        in_specs=in_specs,
        out_specs=[
            pl.BlockSpec((DEC_SEQ, GLA_HV), lambda b: (b, 0)),
            pl.BlockSpec(st_block, lambda b: (layer, b, 0, 0, 0)),
        ],
        out_shape=[
            jax.ShapeDtypeStruct((NS_TOK, GLA_HV), BF16),
            jax.ShapeDtypeStruct((n_a, DEC_BATCH, GLA_H, GLA_DK, GLA_DV), F32),
        ],
        input_output_aliases=aliases,
        compiler_params=_params(("arbitrary",)),
        name="gla_sample",
    )(*args)


def _rope_partner(x):
    lane = lax.broadcasted_iota(jnp.int32, x.shape, 1)
    return jnp.where((lane & (MLA_ROPE - 1)) < MLA_ROPE // 2, pltpu.roll(x, x.shape[1] - MLA_ROPE // 2, 1),
                     pltpu.roll(x, MLA_ROPE // 2, 1))


def _kvprep_kernel(ckv_ref, kr_ref, g_ref, ckvn_ref, krr_ref, c4_ref, s4_ref, *, np_tiles):
    i = pl.program_id(0)
    x = ckv_ref[...]
    ms = jnp.mean(x * x, axis=-1, keepdims=True)
    ckvn_ref[...] = x * lax.rsqrt(ms + EPS) * g_ref[...]

    r = i * TM + lax.broadcasted_iota(jnp.int32, (TM, 128), 0)
    pos = jnp.where(i < np_tiles, r & (SEQ - 1), PAST + (r & (DEC_SEQ - 1))).astype(F32)
    k2 = (lax.broadcasted_iota(jnp.int32, (8, 128), 1) & (MLA_ROPE // 2 - 1)) * 2
    inv = jnp.power(jnp.float32(ROPE_BASE), -k2.astype(F32) / MLA_ROPE)[0:1, :]
    ang = pos * inv
    lane = lax.broadcasted_iota(jnp.int32, (TM, 128), 1)
    c4 = jnp.cos(ang)
    s4 = jnp.where((lane & (MLA_ROPE - 1)) < MLA_ROPE // 2, -jnp.sin(ang), jnp.sin(ang))
    c4_ref[...] = c4
    s4_ref[...] = s4
    kr = kr_ref[...]
    krr_ref[...] = (kr * c4 + _rope_partner(kr) * s4)[:, :MLA_ROPE]


def mla_kvprep(proj, kv_norm_g):
    np_tiles = NP_TOK // TM
    return pl.pallas_call(
        functools.partial(_kvprep_kernel, np_tiles=np_tiles),
        grid=(N_TOK // TM,),
        in_specs=[
            pl.BlockSpec((TM, MLA_RANK), lambda i: (i, 3072 // MLA_RANK)),
            pl.BlockSpec((TM, 128), lambda i: (i, 5632 // 128)),
            pl.BlockSpec((1, MLA_RANK), lambda i: (0, 0)),
        ],
        out_specs=[
            pl.BlockSpec((TM, MLA_RANK), lambda i: (i, 0)),
            pl.BlockSpec((TM, MLA_ROPE), lambda i: (i, 0)),
            pl.BlockSpec((TM, 128), lambda i: (i, 0)),
            pl.BlockSpec((TM, 128), lambda i: (i, 0)),
        ],
        out_shape=[
            jax.ShapeDtypeStruct((N_TOK, MLA_RANK), F32),
            jax.ShapeDtypeStruct((N_TOK, MLA_ROPE), F32),
            jax.ShapeDtypeStruct((N_TOK, 128), F32),
            jax.ShapeDtypeStruct((N_TOK, 128), F32),
        ],
        compiler_params=_params(("arbitrary",)),
        name="mla_kvprep",
    )(proj, proj, kv_norm_g.reshape(1, MLA_RANK))


def _kvup_kernel(ckv_ref, kr_ref, wuk_ref, wuv_ref, k_ref, v_ref):
    c = ckv_ref[...].astype(BF16)
    kn = _dot(c, wuk_ref[...]).astype(BF16)
    vv = _dot(c, wuv_ref[...]).astype(BF16)
    kr = kr_ref[...].astype(BF16)
    for h in range(MLA_H):
        k_ref[h, :, :MLA_NOPE] = kn[:, h * MLA_NOPE:(h + 1) * MLA_NOPE]
        k_ref[h, :, MLA_NOPE:] = kr
        v_ref[h] = vv[:, h * MLA_VH:(h + 1) * MLA_VH]


def mla_kvup(ckv_n, kr_r, w_uk2, w_uv2):
    return pl.pallas_call(
        _kvup_kernel,
        grid=(NP_TOK // TM,),
        in_specs=[
            pl.BlockSpec((TM, MLA_RANK), lambda i: (i, 0)),
            pl.BlockSpec((TM, MLA_ROPE), lambda i: (i, 0)),
            pl.BlockSpec((MLA_RANK, MLA_H * MLA_NOPE), lambda i: (0, 0)),
            pl.BlockSpec((MLA_RANK, MLA_H * MLA_VH), lambda i: (0, 0)),
        ],
        out_specs=[
            pl.BlockSpec((MLA_H, TM, MLA_QK), lambda i: (0, i, 0)),
            pl.BlockSpec((MLA_H, TM, MLA_VH), lambda i: (0, i, 0)),
        ],
        out_shape=[
            jax.ShapeDtypeStruct((MLA_H, NP_TOK, MLA_QK), BF16),
            jax.ShapeDtypeStruct((MLA_H, NP_TOK, MLA_VH), BF16),
        ],
        compiler_params=_params(("arbitrary",)),
        name="mla_kvup",
    )(ckv_n, kr_r, w_uk2, w_uv2)


FLASH_T = 512


def _rope_pair_select(x2, h):
    return jnp.where(h % 2 == 1, pltpu.roll(x2, MLA_ROPE, 1), x2)[:, :MLA_ROPE]


FLASH_G = 4


def _lanes(x, n):
    return x if n == 128 else jnp.concatenate([x] * (n // 128), axis=1)


def _flash_kernel(qt_ref, kt_ref, qn_ref, qr_ref, c4_ref, s4_ref, k_ref, v_ref, gate_ref, o_ref,
                  q_sc, m_sc, l_sc, acc_sc):
    s_id = pl.program_id(2)
    qi = qt_ref[s_id]
    ki = kt_ref[s_id]
    T, G = FLASH_T, FLASH_G

    @pl.when(ki == 0)
    def _():
        x = qr_ref[...]
        c = _lanes(c4_ref[...], G * MLA_ROPE)
        s = _lanes(s4_ref[...], G * MLA_ROPE)
        rot = (x * c + _rope_partner(x) * s) * MLA_SCALE
        for g in range(G):
            q_sc[g, :, :MLA_NOPE] = (qn_ref[:, g * MLA_NOPE:(g + 1) * MLA_NOPE] * MLA_SCALE).astype(BF16)
            q_sc[g, :, MLA_NOPE:] = rot[:, g * MLA_ROPE:(g + 1) * MLA_ROPE].astype(BF16)
        m_sc[...] = jnp.full_like(m_sc, _NEG)
        l_sc[...] = jnp.zeros_like(l_sc)
        acc_sc[...] = jnp.zeros_like(acc_sc)

    def step(masked):
        for g in range(G):
            s = _dot_nt(q_sc[g], k_ref[g])
            if masked:
                r = lax.broadcasted_iota(jnp.int32, (T, T), 0)
                c = lax.broadcasted_iota(jnp.int32, (T, T), 1)
                s = jnp.where(c <= r, s, _NEG)
            m_prev = m_sc[g]
            m_new = jnp.maximum(m_prev, jnp.max(s, axis=-1, keepdims=True))
            alpha = jnp.exp(m_prev - m_new)
            p = jnp.exp(s - _lanes(m_new, T))
            l_sc[g] = alpha * l_sc[g] + jnp.sum(p, axis=-1, keepdims=True)
            acc_sc[g] = alpha * acc_sc[g] + _dot(p.astype(BF16), v_ref[g])
            m_sc[g] = m_new

    @pl.when(ki < qi)
    def _():
        step(False)

    @pl.when(ki == qi)
    def _():
        step(True)
        for g in range(G):
            sl = slice(g * MLA_VH, (g + 1) * MLA_VH)
            o = acc_sc[g] / l_sc[g]
            o_ref[:, sl] = (o * _silu(gate_ref[:, sl])).astype(BF16)


def mla_flash(proj, c4, s4, kcat, vv):
    T, G = FLASH_T, FLASH_G
    nq = SEQ // T
    pairs = [(qi, ki) for qi in range(nq) for ki in range(qi + 1)]
    qt = jnp.asarray(np.array([p[0] for p in pairs], np.int32))
    kt = jnp.asarray(np.array([p[1] for p in pairs], np.int32))

    def qrow(b, g, s, qt, kt):
        return b * nq + qt[s]

    def krow(b, g, s, qt, kt):
        return b * nq + kt[s]

    grid_spec = pltpu.PrefetchScalarGridSpec(
        num_scalar_prefetch=2,
        grid=(BATCH, MLA_H // G, len(pairs)),
        in_specs=[
            pl.BlockSpec((T, G * MLA_NOPE), lambda b, g, s, qt, kt: (qrow(b, g, s, qt, kt), g)),
            pl.BlockSpec((T, G * MLA_ROPE),
                         lambda b, g, s, qt, kt: (qrow(b, g, s, qt, kt), 2048 // (G * MLA_ROPE) + g)),
            pl.BlockSpec((T, 128), lambda b, g, s, qt, kt: (qrow(b, g, s, qt, kt), 0)),
            pl.BlockSpec((T, 128), lambda b, g, s, qt, kt: (qrow(b, g, s, qt, kt), 0)),
            pl.BlockSpec((G, T, MLA_QK), lambda b, g, s, qt, kt: (g, krow(b, g, s, qt, kt), 0)),
            pl.BlockSpec((G, T, MLA_VH), lambda b, g, s, qt, kt: (g, krow(b, g, s, qt, kt), 0)),
            pl.BlockSpec((T, G * MLA_VH),
                         lambda b, g, s, qt, kt: (qrow(b, g, s, qt, kt), 3584 // (G * MLA_VH) + g)),
        ],
        out_specs=pl.BlockSpec((T, G * MLA_VH), lambda b, g, s, qt, kt: (qrow(b, g, s, qt, kt), g)),
        scratch_shapes=[
            pltpu.VMEM((G, T, MLA_QK), BF16),
            pltpu.VMEM((G, T, 128), F32),
            pltpu.VMEM((G, T, 128), F32),
            pltpu.VMEM((G, T, MLA_VH), F32),
        ],
    )
    return pl.pallas_call(
        _flash_kernel,
        grid_spec=grid_spec,
        out_shape=jax.ShapeDtypeStruct((NP_TOK, MLA_H * MLA_VH), BF16),
        compiler_params=_params(("arbitrary", "arbitrary", "arbitrary")),
        name="mla_flash",
    )(qt, kt, proj, proj, c4, s4, kcat, vv, proj)


def _qabs_kernel(qn_ref, qr_ref, c4_ref, s4_ref, wukt_ref, o_ref):
    h = pl.program_id(0)
    ql = _dot(qn_ref[...].astype(BF16), wukt_ref[0]) * MLA_SCALE
    x2 = qr_ref[...]
    rot = x2 * c4_ref[...] + _rope_partner(x2) * s4_ref[...]
    qr = _rope_pair_select(rot, h) * MLA_SCALE
    o_ref[:, 0, :, :MLA_RANK] = ql.reshape(DEC_BATCH, DEC_SEQ, MLA_RANK)
    o_ref[:, 0, :, MLA_RANK:] = qr.reshape(DEC_BATCH, DEC_SEQ, MLA_ROPE)


def mla_qabs(proj, c4, s4, w_ukt3):
    rb = NP_TOK // NS_TOK
    return pl.pallas_call(
        _qabs_kernel,
        grid=(MLA_H,),
        in_specs=[
            pl.BlockSpec((NS_TOK, 128), lambda h: (rb, h)),
            pl.BlockSpec((NS_TOK, 128), lambda h: (rb, 2048 // 128 + h // 2)),
            pl.BlockSpec((NS_TOK, 128), lambda h: (rb, 0)),
            pl.BlockSpec((NS_TOK, 128), lambda h: (rb, 0)),
            pl.BlockSpec((1, MLA_NOPE, MLA_RANK), lambda h: (h, 0, 0)),
        ],
        out_specs=pl.BlockSpec((DEC_BATCH, 1, DEC_SEQ, MLA_LAT), lambda h: (0, h, 0, 0)),
        out_shape=jax.ShapeDtypeStruct((DEC_BATCH, MLA_H, DEC_SEQ, MLA_LAT), F32),
        compiler_params=_params(("arbitrary",)),
        name="mla_qabs",
    )(proj, proj, c4, s4, w_ukt3)


DEC_SEQS = 2
DEC_PG = 8
DEC_NG = N_PAGES // DEC_PG
DEC_SLOTS = 2


def _decode_kernel(pt_ref, q_ref, cnew_ref, knew_ref, ckv_hbm, krt_hbm, o_ref,
                   kv_buf, kr_buf, sem, m_sc, l_sc, acc_sc, *, layer):
    step = pl.program_id(0)
    n_groups = pl.num_programs(0) * DEC_NG
    R = MLA_H * DEC_SEQ

    def group_copies(n, slot):
        first_page = lax.div(n, DEC_NG) * (DEC_SEQS * N_PAGES) + lax.rem(n, DEC_NG) * DEC_PG
        cps = []
        for s in range(DEC_SEQS):
            for r in range(DEC_PG):
                page = pt_ref[first_page + s * N_PAGES + r]
                j = s * DEC_PG + r
                cps.append(pltpu.make_async_copy(ckv_hbm.at[layer, page], kv_buf.at[slot, j], sem.at[0, slot]))
                cps.append(pltpu.make_async_copy(krt_hbm.at[layer, page], kr_buf.at[slot, j], sem.at[1, slot]))
        return cps

    @pl.when(step == 0)
    def _():
        for cp in group_copies(0, 0):
            cp.start()

    m_sc[...] = jnp.full_like(m_sc, _NEG)
    l_sc[...] = jnp.zeros_like(l_sc)
    acc_sc[...] = jnp.zeros_like(acc_sc)

    ql, qr = [], []
    for s in range(DEC_SEQS):
        q = q_ref[s].reshape(R, MLA_LAT)
        ql.append(q[:, :MLA_RANK].astype(BF16))
        qr.append(q[:, MLA_RANK:].astype(BF16))

    def online(s, scores, vals):
        m_prev = m_sc[s]
        m_new = jnp.maximum(m_prev, jnp.max(scores, axis=-1, keepdims=True))
        alpha = jnp.exp(m_prev - m_new)
        p = jnp.exp(scores - _lanes(m_new, scores.shape[1]))
        l_sc[s] = alpha * l_sc[s] + jnp.sum(p, axis=-1, keepdims=True)
        acc_sc[s] = _lanes(alpha, MLA_RANK) * acc_sc[s] + _dot(p.astype(BF16), vals)
        m_sc[s] = m_new

    def group(g, carry):
        n = step * DEC_NG + g
        slot = lax.rem(n, DEC_SLOTS)
        for cp in group_copies(n, slot):
            cp.wait()

        @pl.when(n + 1 < n_groups)
        def _():
            for cp in group_copies(n + 1, 1 - slot):
                cp.start()

        for s in range(DEC_SEQS):
            kv = jnp.concatenate(
                [kv_buf[slot, s * DEC_PG + r].astype(BF16) for r in range(DEC_PG)], axis=0)
            krt = jnp.concatenate(
                [kr_buf[slot, s * DEC_PG + r].astype(BF16) for r in range(DEC_PG)], axis=1)
            online(s, _dot_nt(ql[s], kv) + _dot(qr[s], krt), kv)
        return carry

    lax.fori_loop(0, DEC_NG, group, 0)

    t = lax.broadcasted_iota(jnp.int32, (R, PAGE), 0) & (DEC_SEQ - 1)
    j = lax.broadcasted_iota(jnp.int32, (R, PAGE), 1)
    for s in range(DEC_SEQS):
        rows = slice(s * DEC_SEQ, (s + 1) * DEC_SEQ)
        cn = jnp.concatenate([cnew_ref[rows, :], jnp.zeros((PAGE - DEC_SEQ, MLA_RANK), F32)], axis=0).astype(BF16)
        kn = jnp.concatenate([knew_ref[rows, :], jnp.zeros((PAGE - DEC_SEQ, MLA_ROPE), F32)], axis=0).astype(BF16)
        sc = _dot_nt(ql[s], cn) + _dot_nt(qr[s], kn)
        online(s, jnp.where(j <= t, sc, _NEG), cn)
        o = acc_sc[s] / _lanes(l_sc[s], MLA_RANK)
        o_ref[s] = o.reshape(MLA_H, DEC_SEQ, MLA_RANK)


def mla_decode(qcat, cache_ckv, cache_krt, layer, page_table, ckv_n, kr_r):
    rows = DEC_SEQS * DEC_SEQ
    r0 = NP_TOK // rows
    n_pg = DEC_SEQS * DEC_PG
    R = MLA_H * DEC_SEQ
    grid_spec = pltpu.PrefetchScalarGridSpec(
        num_scalar_prefetch=1,
        grid=(DEC_BATCH // DEC_SEQS,),
        in_specs=[
            pl.BlockSpec((DEC_SEQS, MLA_H, DEC_SEQ, MLA_LAT), lambda i, pt: (i, 0, 0, 0)),
            pl.BlockSpec((rows, MLA_RANK), lambda i, pt: (r0 + i, 0)),
            pl.BlockSpec((rows, MLA_ROPE), lambda i, pt: (r0 + i, 0)),
            pl.BlockSpec(memory_space=pl.ANY),
            pl.BlockSpec(memory_space=pl.ANY),
        ],
        out_specs=pl.BlockSpec((DEC_SEQS, MLA_H, DEC_SEQ, MLA_RANK), lambda i, pt: (i, 0, 0, 0)),
        scratch_shapes=[
            pltpu.VMEM((DEC_SLOTS, n_pg, PAGE, MLA_RANK), F32),
            pltpu.VMEM((DEC_SLOTS, n_pg, MLA_ROPE, PAGE), F32),
            pltpu.SemaphoreType.DMA((2, DEC_SLOTS)),
            pltpu.VMEM((DEC_SEQS, R, 128), F32),
            pltpu.VMEM((DEC_SEQS, R, 128), F32),
            pltpu.VMEM((DEC_SEQS, R, MLA_RANK), F32),
        ],
    )
    return pl.pallas_call(
        functools.partial(_decode_kernel, layer=layer),
        grid_spec=grid_spec,
        out_shape=jax.ShapeDtypeStruct((DEC_BATCH, MLA_H, DEC_SEQ, MLA_RANK), F32),
        compiler_params=_params(("arbitrary",)),
        name="mla_decode",
    )(page_table.reshape(-1), qcat, ckv_n, kr_r, cache_ckv, cache_krt)


def _uvup_kernel(ol_ref, wuv_ref, gate_ref, a_ref):
    ol = ol_ref[...].reshape(NS_TOK, MLA_RANK).astype(BF16)
    a_ref[...] = (_dot(ol, wuv_ref[0]) * _silu(gate_ref[...])).astype(BF16)


def mla_uvup(o_lat, w_uv3, proj):
    rb = NP_TOK // NS_TOK
    return pl.pallas_call(
        _uvup_kernel,
        grid=(MLA_H,),
        in_specs=[
            pl.BlockSpec((DEC_BATCH, 1, DEC_SEQ, MLA_RANK), lambda h: (0, h, 0, 0)),
            pl.BlockSpec((1, MLA_RANK, MLA_VH), lambda h: (h, 0, 0)),
            pl.BlockSpec((NS_TOK, 128), lambda h: (rb, 3584 // 128 + h)),
        ],
        out_specs=pl.BlockSpec((NS_TOK, MLA_VH), lambda h: (0, h)),
        out_shape=jax.ShapeDtypeStruct((NS_TOK, MLA_H * MLA_VH), BF16),
        compiler_params=_params(("arbitrary",)),
        name="mla_uvup",
    )(o_lat, w_uv3, proj)


POOL_HALO = 16


def _pool_prompt_kernel(u_ref, halo_ref, gate_ref, wg_ref, sc_ref, a_ref, tail_ref, ext_sc, *, tiles_per_seq):
    i = pl.program_id(0)
    g = pl.program_id(1)
    first = (i % tiles_per_seq) == 0
    u = u_ref[...]
    tail_ref[0] = u[TM - POOL_HALO:, :]
    ext_sc[:POOL_HALO, :] = jnp.where(first, 0.0, halo_ref[...])
    ext_sc[POOL_HALO:, :] = u
    t = (i % tiles_per_seq) * TM + lax.broadcasted_iota(jnp.int32, (TM, 1), 0)

    for gi, w in enumerate(POOL_WINDOWS):
        @pl.when(g == gi)
        def _(w=w):
            acc = u
            for j in range(1, w):
                acc = acc + ext_sc[POOL_HALO - j:POOL_HALO - j + TM, :]
            cnt = jnp.minimum(t + 1, w).astype(F32)
            p = (acc / cnt - u).astype(BF16)
            z = _dot(p, wg_ref[0]) * sc_ref[...]
            a_ref[...] = (z * _silu(gate_ref[...])).astype(BF16)


def pool_prompt(proj, w_grp, pscale):
    tiles_per_seq = SEQ // TM
    hb = TM // POOL_HALO
    return pl.pallas_call(
        functools.partial(_pool_prompt_kernel, tiles_per_seq=tiles_per_seq),
        grid=(NP_TOK // TM, len(POOL_WINDOWS)),
        in_specs=[
            pl.BlockSpec((TM, POOL_G), lambda i, g: (i, g)),
            pl.BlockSpec((POOL_HALO, POOL_G), lambda i, g: (jnp.maximum(i * hb - 1, 0), g)),
            pl.BlockSpec((TM, POOL_G), lambda i, g: (i, len(POOL_WINDOWS) + g)),
            pl.BlockSpec((1, POOL_G, POOL_G), lambda i, g: (g, 0, 0)),
            pl.BlockSpec((1, POOL_G), lambda i, g: (0, g)),
        ],
        out_specs=[
            pl.BlockSpec((TM, POOL_G), lambda i, g: (i, g)),
            pl.BlockSpec((1, POOL_HALO, POOL_G), lambda i, g: (i, 0, g)),
        ],
        out_shape=[
            jax.ShapeDtypeStruct((NP_TOK, D), BF16),
            jax.ShapeDtypeStruct((NP_TOK // TM, POOL_HALO, D), F32),
        ],
        scratch_shapes=[pltpu.VMEM((POOL_HALO + TM, POOL_G), F32)],
        compiler_params=_params(("arbitrary", "arbitrary")),
        name="pool_prompt",
    )(proj, proj, proj, w_grp, pscale)


def _pool_sample_kernel(u_ref, gate_ref, hist_ref, wg_ref, sc_ref, a_ref):
    g = pl.program_id(0)

    def seq(r):
        return hist_ref[r] if r < POOL_HIST else u_ref[r - POOL_HIST]

    for gi, w in enumerate(POOL_WINDOWS):
        @pl.when(g == gi)
        def _(w=w):
            ps = []
            for t in range(DEC_SEQ):
                acc = seq(POOL_HIST + t)
                for j in range(1, w):
                    acc = acc + seq(POOL_HIST + t - j)
                ps.append(acc / float(w) - u_ref[t])
            p = jnp.concatenate(ps, axis=0).astype(BF16)
            z = _dot(p, wg_ref[0]) * sc_ref[...]
            gate = gate_ref[...].reshape(NS_TOK, POOL_G)
            a_ref[...] = (z * _silu(gate)).astype(BF16).reshape(DEC_SEQ, DEC_BATCH, POOL_G)


def pool_sample(proj_t, hist_t, w_grp, pscale):
    ng = len(POOL_WINDOWS)
    return pl.pallas_call(
        _pool_sample_kernel,
        grid=(ng,),
        in_specs=[
            pl.BlockSpec((DEC_SEQ, DEC_BATCH, POOL_G), lambda g: (0, 0, g)),
            pl.BlockSpec((DEC_SEQ, DEC_BATCH, POOL_G), lambda g: (0, 0, ng + g)),
            pl.BlockSpec((POOL_HIST, DEC_BATCH, POOL_G), lambda g: (0, 0, g)),
            pl.BlockSpec((1, POOL_G, POOL_G), lambda g: (g, 0, 0)),
            pl.BlockSpec((1, POOL_G), lambda g: (0, g)),
        ],
        out_specs=pl.BlockSpec((DEC_SEQ, DEC_BATCH, POOL_G), lambda g: (0, 0, g)),
        out_shape=jax.ShapeDtypeStruct((DEC_SEQ, DEC_BATCH, D), BF16),
        compiler_params=_params(("arbitrary",)),
        name="pool_sample",
    )(proj_t, proj_t, hist_t, w_grp, pscale)


def kernel(x_prompt, x_sample, c_prompt, c_sample, state_gla, cache_ckv, cache_kr, state_pool, page_table, norm_g, ada_w, ada_b, final_norm_g, gla_w_in, gla_w_gate_up, gla_b_gate, gla_onorm_g, gla_w_out, mla_w_in, mla_kv_norm_g, mla_w_uk, mla_w_uv, mla_w_out, pool_w_in, pool_w_grp, pool_scale, pool_w_out):
    x_all = jnp.concatenate([x_prompt.reshape(NP_TOK, D), x_sample.reshape(NS_TOK, D)], axis=0)

    n_c = BATCH + DEC_BATCH
    c_all = jnp.concatenate([c_prompt, c_sample, jnp.zeros((8 - n_c % 8, D), F32)], axis=0)
    mod = ada_mod(c_all, ada_w, ada_b)

    gla_states_p, gla_states_s = [], None
    ckv_rows, kr_rows, pool_p, pool_s = [], [], [], []
    ia = ib = ic = 0
    y_all = None
    for l in range(DEPTH):
        mod_p = mod[l, :BATCH].reshape(BATCH, 1, 3 * D)
        mod_s = jnp.repeat(mod[l, BATCH:n_c], DEC_SEQ, axis=0)
        mixer = LAYER_MIXER[l]
        if mixer == 0:
            w_in = gla_w_in[ia]
            cut = 2 * GLA_HK + GLA_HV
            w_main = jnp.concatenate([w_in[:, :cut], w_in[:, cut + GLA_RANK:]], axis=1).astype(BF16)
            w_glr = jnp.pad(w_in[:, cut:cut + GLA_RANK], ((0, 0), (0, 128 - GLA_RANK))).astype(BF16)
            w_up = jnp.pad(gla_w_gate_up[ia], ((0, 128 - GLA_RANK), (0, 0))).astype(BF16)
            proj, gk = norm_proj(x_all, norm_g[l], mod_p, mod_s, w_main, 1024,
                                 gk_weights=(w_glr, w_up, gla_b_gate[ia].reshape(1, GLA_HK)))
            a_p, st_p = gla_prompt(proj, gk, gla_onorm_g[ia])
            a_s, gla_states_s = gla_sample(proj, gk, gla_onorm_g[ia], state_gla, ia, states_out=gla_states_s)
            gla_states_p.append(st_p)
            w_out = gla_w_out[ia].astype(BF16)
            ia += 1
        elif mixer == 1:
            w_in = mla_w_in[ib]
            nq = MLA_H * MLA_QK
            wq = w_in[:, :nq].reshape(D, MLA_H, MLA_QK)
            w_perm = jnp.concatenate([
                wq[:, :, :MLA_NOPE].reshape(D, MLA_H * MLA_NOPE),
                wq[:, :, MLA_NOPE:].reshape(D, MLA_H * MLA_ROPE),
                w_in[:, nq:nq + MLA_RANK],
                w_in[:, nq + MLA_RANK + MLA_ROPE:],
                w_in[:, nq + MLA_RANK:nq + MLA_RANK + MLA_ROPE],
                jnp.zeros((D, 128 - MLA_ROPE), F32),
            ], axis=1).astype(BF16)
            proj = norm_proj(x_all, norm_g[l], mod_p, mod_s, w_perm, 1152)
            ckv_n, kr_r, c4, s4 = mla_kvprep(proj, mla_kv_norm_g[ib])
            w_uk2 = mla_w_uk[ib].reshape(MLA_RANK, MLA_H * MLA_NOPE).astype(BF16)
            w_uv2 = mla_w_uv[ib].reshape(MLA_RANK, MLA_H * MLA_VH).astype(BF16)
            w_ukt3 = jnp.transpose(mla_w_uk[ib], (1, 2, 0)).astype(BF16)
            w_uv3 = jnp.transpose(mla_w_uv[ib], (1, 0, 2)).astype(BF16)
            kcat, vv = mla_kvup(ckv_n, kr_r, w_uk2, w_uv2)
            a_p = mla_flash(proj, c4, s4, kcat, vv)
            qcat = mla_qabs(proj, c4, s4, w_ukt3)
            cache_krt = jnp.swapaxes(cache_kr, 2, 3)
            o_lat = mla_decode(qcat, cache_ckv, cache_krt, ib, page_table, ckv_n, kr_r)
            a_s = mla_uvup(o_lat, w_uv3, proj)
            ckv_rows.append(ckv_n)
            kr_rows.append(kr_r)
            w_out = mla_w_out[ib].astype(BF16)
            ib += 1
        else:
            proj = norm_proj(x_all, norm_g[l], mod_p, mod_s, pool_w_in[ic].astype(BF16), 1024)
            w_grp = pool_w_grp[ic].astype(BF16)
            pscale = pool_scale[ic].reshape(1, D)
            a_p, u_tails = pool_prompt(proj, w_grp, pscale)
            proj_t = jnp.transpose(proj[NP_TOK:].reshape(DEC_BATCH, DEC_SEQ, 2 * D), (1, 0, 2))
            hist_t = jnp.transpose(state_pool[ic], (1, 0, 2))
            a_st = pool_sample(proj_t, hist_t, w_grp, pscale)
            a_s = jnp.transpose(a_st, (1, 0, 2)).reshape(NS_TOK, D)
            u_s = proj[NP_TOK:, :D].reshape(DEC_BATCH, DEC_SEQ, D)
            pool_p.append(u_tails.reshape(BATCH, SEQ // TM, POOL_HALO, D)[:, -1, POOL_HALO - POOL_HIST:, :])
            pool_s.append(jnp.concatenate([state_pool[ic][:, DEC_SEQ:, :], u_s], axis=1))
            w_out = pool_w_out[ic].astype(BF16)
            ic += 1
        if l == DEPTH - 1:
            y_p, y_s = out_proj(a_p, a_s, w_out, x_all, mod_p, mod_s, final_g=final_norm_g)
        else:
            x_all = out_proj(a_p, a_s, w_out, x_all, mod_p, mod_s)

    y_prompt = y_p.reshape(BATCH, SEQ, D)
    y_sample = y_s.reshape(DEC_BATCH, DEC_SEQ, D)
    ckv_all = jnp.stack(ckv_rows)
    kr_all = jnp.stack(kr_rows)
    return (
        y_prompt,
        y_sample,
        jnp.stack(gla_states_p),
        gla_states_s,
        ckv_all[:, :NP_TOK].reshape(-1, BATCH, SEQ, MLA_RANK),
        kr_all[:, :NP_TOK].reshape(-1, BATCH, SEQ, MLA_ROPE),
        ckv_all[:, NP_TOK:].reshape(-1, DEC_BATCH, DEC_SEQ, MLA_RANK),
        kr_all[:, NP_TOK:].reshape(-1, DEC_BATCH, DEC_SEQ, MLA_ROPE),
        jnp.stack(pool_p),
        jnp.stack(pool_s),
    )
```

```python
import functools

import numpy as np
import jax
import jax.numpy as jnp
from jax import lax
from jax.experimental import pallas as pl
from jax.experimental.pallas import tpu as pltpu

F32 = jnp.float32
BF16 = jnp.bfloat16

D = 2048
BATCH = 4
SEQ = 2048
DEC_BATCH = 128
DEC_SEQ = 8
PAGE = 128
N_PAGES = 64
PAST = N_PAGES * PAGE
DEPTH = 4
LAYER_MIXER = (0, 1, 2, 0)
EPS = 1e-6

NP_TOK = BATCH * SEQ
NS_TOK = DEC_BATCH * DEC_SEQ
N_TOK = NP_TOK + NS_TOK

GLA_H = 4
GLA_DK = 256
GLA_DV = 512
GLA_RANK = 16
GLA_TAU = 16.0
GLA_HK = GLA_H * GLA_DK
GLA_HV = GLA_H * GLA_DV
GLA_C = 128
GLA_SUB = 16

MLA_H = 16
MLA_NOPE = 128
MLA_ROPE = 64
MLA_VH = 128
MLA_RANK = 512
MLA_SCALE = (MLA_NOPE + MLA_ROPE) ** -0.5
MLA_QK = MLA_NOPE + MLA_ROPE
MLA_LAT = MLA_RANK + MLA_ROPE
MLA_N = 5760
ROPE_BASE = 10000.0

POOL_WINDOWS = (2, 4, 8, 16)
POOL_G = 512
POOL_HIST = 15

TM = 512
TM_OUT = 256
VMEM_LIMIT = 56 * 1024 * 1024

_NEG = -1e30


def _params(sem):
    return pltpu.CompilerParams(dimension_semantics=sem, vmem_limit_bytes=VMEM_LIMIT)


def _silu(x):
    return x * (1.0 / (1.0 + jnp.exp(-x)))


def _dot(a, b):
    return jnp.dot(a, b, preferred_element_type=F32)


def _dot_nt(a, b):
    return lax.dot_general(a, b, (((1,), (1,)), ((), ())), preferred_element_type=F32)


def _ada_kernel(c_ref, w_ref, b_ref, o_ref):
    sc = _silu(c_ref[...]).astype(BF16)
    o_ref[0] = _dot(sc, w_ref[0].astype(BF16)) + b_ref[0]


def ada_mod(c_all, ada_w, ada_b):
    rows = c_all.shape[0]
    tn = 768
    return pl.pallas_call(
        _ada_kernel,
        grid=(DEPTH, 3 * D // tn),
        in_specs=[
            pl.BlockSpec((rows, D), lambda l, j: (0, 0)),
            pl.BlockSpec((1, D, tn), lambda l, j: (l, 0, j)),
            pl.BlockSpec((1, 1, tn), lambda l, j: (l, 0, j)),
        ],
        out_specs=pl.BlockSpec((1, rows, tn), lambda l, j: (l, 0, j)),
        out_shape=jax.ShapeDtypeStruct((DEPTH, rows, 3 * D), F32),
        compiler_params=_params(("arbitrary", "arbitrary")),
        name="ada_mod",
    )(c_all, ada_w, ada_b.reshape(DEPTH, 1, 3 * D))


def _mod_rows(i, np_tiles, p_ref, s_ref):
    return jnp.where(i < np_tiles, p_ref[0], s_ref[...])


def _norm_mod(x, g, shift, scale):
    ms = jnp.mean(x * x, axis=-1, keepdims=True)
    return ((x * lax.rsqrt(ms + EPS) * g) * (1.0 + scale) + shift).astype(BF16)


def _mod_specs(tm, cols):
    np_tiles = NP_TOK // tm
    tiles_per_seq = SEQ // tm
    specs = []
    for col in cols:
        specs.append(pl.BlockSpec((1, 1, D), lambda i, col=col: (jnp.minimum(i // tiles_per_seq, BATCH - 1), 0, col)))
        specs.append(pl.BlockSpec((tm, D), lambda i, col=col: (jnp.maximum(i - np_tiles, 0), col)))
    return specs


def _norm_kernel(x_ref, g_ref, shp_ref, shs_ref, scp_ref, scs_ref, h_ref, *, np_tiles):
    i = pl.program_id(0)
    h_ref[...] = _norm_mod(x_ref[...], g_ref[...], _mod_rows(i, np_tiles, shp_ref, shs_ref),
                           _mod_rows(i, np_tiles, scp_ref, scs_ref))


def norm_mod(x_all, norm_g, mod_p, mod_s):
    tm = TM
    return pl.pallas_call(
        functools.partial(_norm_kernel, np_tiles=NP_TOK // tm),
        grid=(N_TOK // tm,),
        in_specs=[pl.BlockSpec((tm, D), lambda i: (i, 0)), pl.BlockSpec((1, D), lambda i: (0, 0))]
        + _mod_specs(tm, (0, 1)),
        out_specs=pl.BlockSpec((tm, D), lambda i: (i, 0)),
        out_shape=jax.ShapeDtypeStruct((N_TOK, D), BF16),
        compiler_params=_params(("arbitrary",)),
        name="norm_mod",
    )(x_all, norm_g.reshape(1, D), mod_p, mod_s, mod_p, mod_s)


TMM = 1024


def _mm_kernel(h_ref, w_ref, o_ref, *, w_is_nk):
    o_ref[...] = _dot_nt(h_ref[...], w_ref[...]) if w_is_nk else _dot(h_ref[...], w_ref[...])


def proj_matmul(h, w, tn, w_is_nk):
    n = w.shape[0] if w_is_nk else w.shape[1]
    w_spec = pl.BlockSpec((tn, D), lambda j, i: (j, 0)) if w_is_nk else pl.BlockSpec((D, tn), lambda j, i: (0, j))
    return pl.pallas_call(
        functools.partial(_mm_kernel, w_is_nk=w_is_nk),
        grid=(n // tn, N_TOK // TMM),
        in_specs=[pl.BlockSpec((TMM, D), lambda j, i: (i, 0)), w_spec],
        out_specs=pl.BlockSpec((TMM, tn), lambda j, i: (i, j)),
        out_shape=jax.ShapeDtypeStruct((N_TOK, n), F32),
        compiler_params=_params(("arbitrary", "arbitrary")),
        name="proj_matmul",
    )(h, w)


def _gate_kernel(h_ref, wg_ref, wu_ref, bg_ref, gk_ref):
    glr = _dot_nt(h_ref[...], wg_ref[...]).astype(BF16)
    z = _dot(glr, wu_ref[...]) + bg_ref[...]
    gk_ref[...] = (jnp.minimum(z, 0.0) - jnp.log1p(jnp.exp(-jnp.abs(z)))) / GLA_TAU


def gla_gate(h, wg, wu, bg):
    tm = TM
    return pl.pallas_call(
        _gate_kernel,
        grid=(N_TOK // tm,),
        in_specs=[
            pl.BlockSpec((tm, D), lambda i: (i, 0)),
            pl.BlockSpec(wg.shape, lambda i: (0, 0)),
            pl.BlockSpec(wu.shape, lambda i: (0, 0)),
            pl.BlockSpec(bg.shape, lambda i: (0, 0)),
        ],
        out_specs=pl.BlockSpec((tm, GLA_HK), lambda i: (i, 0)),
        out_shape=jax.ShapeDtypeStruct((N_TOK, GLA_HK), F32),
        compiler_params=_params(("arbitrary",)),
        name="gla_gate",
    )(h, wg, wu, bg)


def _out_kernel(ap_ref, as_ref, w_ref, x_ref, gp_ref, gs_ref, *rest, np_tiles, final):
    i = pl.program_id(0)
    a = jnp.where(i < np_tiles, ap_ref[...], as_ref[...])
    y = _dot(a, w_ref[...])
    gate = _mod_rows(i, np_tiles, gp_ref, gs_ref)
    xn = x_ref[...] + gate * y
    if not final:
        ng_ref, shp_ref, shs_ref, scp_ref, scs_ref, o_ref, h_ref = rest
        o_ref[...] = xn
        h_ref[...] = _norm_mod(xn, ng_ref[...], _mod_rows(i, np_tiles, shp_ref, shs_ref),
                               _mod_rows(i, np_tiles, scp_ref, scs_ref))
        return
    fg_ref, yp_ref, ys_ref = rest
    ms = jnp.mean(xn * xn, axis=-1, keepdims=True)
    yn = xn * lax.rsqrt(ms + EPS) * fg_ref[...]

    @pl.when(i < np_tiles)
    def _():
        yp_ref[...] = yn

    @pl.when(i >= np_tiles)
    def _():
        ys_ref[...] = yn


def out_proj(a_p, a_s, w_out, x_all, mod_p, mod_s, final_g=None, next_norm=None):
    tm = TM_OUT
    np_tiles = NP_TOK // tm
    final = final_g is not None
    in_specs = [
        pl.BlockSpec((tm, D), lambda i: (jnp.minimum(i, np_tiles - 1), 0)),
        pl.BlockSpec((tm, D), lambda i: (jnp.maximum(i - np_tiles, 0), 0)),
        pl.BlockSpec((D, D), lambda i: (0, 0)),
        pl.BlockSpec((tm, D), lambda i: (i, 0)),
    ] + _mod_specs(tm, (2,))
    args = [a_p, a_s, w_out, x_all, mod_p, mod_s]
    if not final:
        ng, nmod_p, nmod_s = next_norm
        in_specs += [pl.BlockSpec((1, D), lambda i: (0, 0))] + _mod_specs(tm, (0, 1))
        args += [ng.reshape(1, D), nmod_p, nmod_s, nmod_p, nmod_s]
        out_specs = [pl.BlockSpec((tm, D), lambda i: (i, 0)), pl.BlockSpec((tm, D), lambda i: (i, 0))]
        out_shape = [jax.ShapeDtypeStruct((N_TOK, D), F32), jax.ShapeDtypeStruct((N_TOK, D), BF16)]
    else:
        in_specs.append(pl.BlockSpec((1, D), lambda i: (0, 0)))
        args.append(final_g.reshape(1, D))
        out_specs = [
            pl.BlockSpec((tm, D), lambda i: (jnp.minimum(i, np_tiles - 1), 0)),
            pl.BlockSpec((tm, D), lambda i: (jnp.maximum(i - np_tiles, 0), 0)),
        ]
        out_shape = [jax.ShapeDtypeStruct((NP_TOK, D), F32), jax.ShapeDtypeStruct((NS_TOK, D), F32)]
    return pl.pallas_call(
        functools.partial(_out_kernel, np_tiles=np_tiles, final=final),
        grid=(N_TOK // tm,),
        in_specs=in_specs,
        out_specs=out_specs,
        out_shape=out_shape,
        compiler_params=_params(("arbitrary",)),
        name="out_proj_final" if final else "out_proj",
    )(*args)


def _cumsum_rows(x):
    n = x.shape[0]
    row = lax.broadcasted_iota(jnp.int32, x.shape, 0)
    s = 1
    while s < n:
        x = x + jnp.where(row >= s, pltpu.roll(x, s, 0), 0.0)
        s *= 2
    return x


LOG2E = 1.4426950408889634


def _cumsum_mxu(x):
    n = x.shape[0]
    r = lax.broadcasted_iota(jnp.int32, (n, n), 0)
    c = lax.broadcasted_iota(jnp.int32, (n, n), 1)
    tri = jnp.where(c <= r, 1.0, 0.0).astype(BF16)
    hi = x.astype(BF16)
    rem = x - hi.astype(F32)
    mid = rem.astype(BF16)
    lo = (rem - mid.astype(F32)).astype(BF16)
    return _dot(tri, hi) + _dot(tri, mid) + _dot(tri, lo)


def _col_from_row(v):
    return jnp.broadcast_to(v, (128, v.shape[1])).T


def _head_rmsnorm_gate(o, g, og):
    ms = jnp.mean(o * o, axis=-1, keepdims=True)
    return (o * lax.rsqrt(ms + EPS) * g) * _silu(og)


def _gla_chunk(q, k, v, gk, S):
    C, SUB = GLA_C, GLA_SUB
    nsub = C // SUB

    cum = _cumsum_mxu(gk) * LOG2E
    excl = cum - gk * LOG2E
    last = cum[C - 1:C, :]

    o = _dot((q * jnp.exp2(cum)).astype(BF16), S.astype(BF16))

    row_blocks = [jnp.zeros((SUB, C), F32)]
    for i in range(1, nsub):
        sl = slice(i * SUB, (i + 1) * SUB)
        n_k = i * SUB
        b_i = excl[n_k:n_k + 1, :]
        qt = (q[sl] * jnp.exp2(cum[sl] - b_i)).astype(BF16)
        kh = (k[:n_k] * jnp.exp2(b_i - cum[:n_k])).astype(BF16)
        kh = jnp.concatenate([kh, jnp.zeros((C - n_k, GLA_DK), BF16)], axis=0)
        row_blocks.append(_dot_nt(qt, kh))
    att = jnp.concatenate(row_blocks, axis=0)

    parts = []
    for d in range(SUB):
        k_d = k if d == 0 else pltpu.roll(k, d, 0)
        cum_d = cum if d == 0 else pltpu.roll(cum, d, 0)
        parts.append((q * k_d * jnp.exp2(cum - cum_d)).astype(BF16))
    ones = jnp.ones((GLA_DK, 128), BF16)
    band = _dot(jnp.concatenate(parts, axis=0), ones)
    r = lax.broadcasted_iota(jnp.int32, (C, C), 0)
    cc = lax.broadcasted_iota(jnp.int32, (C, C), 1)
    off = jnp.where((r & -SUB) == (cc & -SUB), r - cc, -1)
    for d in range(SUB):
        att = jnp.where(off == d, band[d * C:(d + 1) * C], att)

    o = o + _dot(att.astype(BF16), v)

    kd = (k * jnp.exp2(last - cum)).astype(BF16)
    dec = _col_from_row(jnp.exp2(last))
    dec = jnp.concatenate([dec] * (GLA_DV // 128), axis=1)
    return o, dec * S + _dot(kd.T, v)


def _gla_prompt_kernel(qk_ref, v_ref, og_ref, gk_ref, g_ref, a_ref, s_ref):
    @pl.when(pl.program_id(1) == 0)
    def _():
        s_ref[...] = jnp.zeros_like(s_ref)

    for h in range(GLA_H):
        ksl = slice(h * GLA_DK, (h + 1) * GLA_DK)
        vsl = slice(h * GLA_DV, (h + 1) * GLA_DV)
        q = qk_ref[:, ksl] * (GLA_DK ** -0.5)
        k = qk_ref[:, GLA_HK + h * GLA_DK:GLA_HK + (h + 1) * GLA_DK]
        o, s_new = _gla_chunk(q, k, v_ref[:, vsl].astype(BF16), gk_ref[:, ksl], s_ref[0, h])
        s_ref[0, h] = s_new
        a_ref[:, vsl] = _head_rmsnorm_gate(o, g_ref[...], og_ref[:, vsl]).astype(BF16)


def gla_prompt(proj, gk, onorm_g):
    nc = SEQ // GLA_C

    def rows(b, c):
        return b * nc + c

    return pl.pallas_call(
        _gla_prompt_kernel,
        grid=(BATCH, nc),
        in_specs=[
            pl.BlockSpec((GLA_C, 2 * GLA_HK), lambda b, c: (rows(b, c), 0)),
            pl.BlockSpec((GLA_C, GLA_HV), lambda b, c: (rows(b, c), 1)),
            pl.BlockSpec((GLA_C, GLA_HV), lambda b, c: (rows(b, c), 2)),
            pl.BlockSpec((GLA_C, GLA_HK), lambda b, c: (rows(b, c), 0)),
            pl.BlockSpec((1, GLA_DV), lambda b, c: (0, 0)),
        ],
        out_specs=[
            pl.BlockSpec((GLA_C, GLA_HV), lambda b, c: (rows(b, c), 0)),
            pl.BlockSpec((1, GLA_H, GLA_DK, GLA_DV), lambda b, c: (b, 0, 0, 0)),
        ],
        out_shape=[
            jax.ShapeDtypeStruct((NP_TOK, GLA_HV), BF16),
            jax.ShapeDtypeStruct((BATCH, GLA_H, GLA_DK, GLA_DV), F32),
        ],
        compiler_params=_params(("arbitrary", "arbitrary")),
        name="gla_prompt",
    )(proj, proj, proj, gk, onorm_g.reshape(1, GLA_DV))


def _gla_sample_kernel(qk_ref, v_ref, og_ref, gk_ref, g_ref, s0_ref, *rest, aliased):
    a_ref, s_ref = rest[1:] if aliased else rest
    T = DEC_SEQ
    row = lax.broadcasted_iota(jnp.int32, (T, GLA_DV), 0)
    outs = []
    for h in range(GLA_H):
        ksl = slice(h * GLA_DK, (h + 1) * GLA_DK)
        vsl = slice(h * GLA_DV, (h + 1) * GLA_DV)
        q = qk_ref[:, ksl] * (GLA_DK ** -0.5)
        k = qk_ref[:, GLA_HK + h * GLA_DK:GLA_HK + (h + 1) * GLA_DK]
        v = v_ref[:, vsl]
        gk = gk_ref[:, ksl]
        S = s0_ref[0, 0, h]

        cum = _cumsum_rows(gk)
        last = cum[T - 1:T, :]
        o = _dot((q * jnp.exp(cum)).astype(BF16), S.astype(BF16))
        for d in range(T):
            k_d = k if d == 0 else pltpu.roll(k, d, 0)
            cum_d = cum if d == 0 else pltpu.roll(cum, d, 0)
            v_d = v if d == 0 else pltpu.roll(v, d, 0)
            w = jnp.sum(q * k_d * jnp.exp(jnp.minimum(cum - cum_d, 0.0)), axis=-1, keepdims=True)
            o = o + jnp.where(row >= d, w * v_d, 0.0)

        kd = k * jnp.exp(last - cum)
        stacked = jnp.concatenate(
            [kd, jnp.broadcast_to(jnp.exp(last), (T, GLA_DK)), jnp.zeros((128 - 2 * T, GLA_DK), F32)], axis=0)
        st = stacked.T
        vpad = jnp.concatenate([v, jnp.zeros((128 - T, GLA_DV), F32)], axis=0)
        lane = lax.broadcasted_iota(jnp.int32, (GLA_DK, 128), 1)
        kdt = jnp.where(lane < T, st, 0.0).astype(BF16)
        s_ref[0, 0, h] = st[:, T:T + 1] * S + _dot(kdt, vpad.astype(BF16))

        outs.append(_head_rmsnorm_gate(o, g_ref[...], og_ref[:, vsl]))
    a_ref[...] = jnp.concatenate(outs, axis=1).astype(BF16)


def gla_sample(proj, gk, onorm_g, state_all, layer, states_out=None):
    r0 = NP_TOK // DEC_SEQ
    n_a = state_all.shape[0]
    st_block = (1, 1, GLA_H, GLA_DK, GLA_DV)
    in_specs = [
        pl.BlockSpec((DEC_SEQ, 2 * GLA_HK), lambda b: (r0 + b, 0)),
        pl.BlockSpec((DEC_SEQ, GLA_HV), lambda b: (r0 + b, 1)),
        pl.BlockSpec((DEC_SEQ, GLA_HV), lambda b: (r0 + b, 2)),
        pl.BlockSpec((DEC_SEQ, GLA_HK), lambda b: (r0 + b, 0)),
        pl.BlockSpec((1, GLA_DV), lambda b: (0, 0)),
        pl.BlockSpec(st_block, lambda b: (layer, b, 0, 0, 0)),
    ]
    args = [proj, proj, proj, gk, onorm_g.reshape(1, GLA_DV), state_all]
    aliases = {}
    if states_out is not None:
        in_specs.append(pl.BlockSpec(memory_space=pl.ANY))
        args.append(states_out)
        aliases = {len(args) - 1: 1}
    return pl.pallas_call(
        functools.partial(_gla_sample_kernel, aliased=states_out is not None),
        grid=(DEC_BATCH,),
        in_specs=in_specs,
        out_specs=[
            pl.BlockSpec((DEC_SEQ, GLA_HV), lambda b: (b, 0)),
            pl.BlockSpec(st_block, lambda b: (layer, b, 0, 0, 0)),
        ],
        out_shape=[
            jax.ShapeDtypeStruct((NS_TOK, GLA_HV), BF16),
            jax.ShapeDtypeStruct((n_a, DEC_BATCH, GLA_H, GLA_DK, GLA_DV), F32),
        ],
        input_output_aliases=aliases,
        compiler_params=_params(("arbitrary",)),
        name="gla_sample",
    )(*args)


def _rope_partner(x):
    lane = lax.broadcasted_iota(jnp.int32, x.shape, 1)
    return jnp.where((lane & (MLA_ROPE - 1)) < MLA_ROPE // 2, pltpu.roll(x, x.shape[1] - MLA_ROPE // 2, 1),
                     pltpu.roll(x, MLA_ROPE // 2, 1))


def _kvprep_kernel(ckv_ref, kr_ref, g_ref, ckvn_ref, krr_ref, c4_ref, s4_ref, *, np_tiles):
    i = pl.program_id(0)
    x = ckv_ref[...]
    ms = jnp.mean(x * x, axis=-1, keepdims=True)
    ckvn_ref[...] = x * lax.rsqrt(ms + EPS) * g_ref[...]

    r = i * TM + lax.broadcasted_iota(jnp.int32, (TM, 128), 0)
    pos = jnp.where(i < np_tiles, r & (SEQ - 1), PAST + (r & (DEC_SEQ - 1))).astype(F32)
    k2 = (lax.broadcasted_iota(jnp.int32, (8, 128), 1) & (MLA_ROPE // 2 - 1)) * 2
    inv = jnp.power(jnp.float32(ROPE_BASE), -k2.astype(F32) / MLA_ROPE)[0:1, :]
    ang = pos * inv
    lane = lax.broadcasted_iota(jnp.int32, (TM, 128), 1)
    c4 = jnp.cos(ang)
    s4 = jnp.where((lane & (MLA_ROPE - 1)) < MLA_ROPE // 2, -jnp.sin(ang), jnp.sin(ang))
    c4_ref[...] = c4
    s4_ref[...] = s4
    kr = kr_ref[...]
    krr_ref[...] = (kr * c4 + _rope_partner(kr) * s4)[:, :MLA_ROPE]


def mla_kvprep(proj, kv_norm_g):
    np_tiles = NP_TOK // TM
    return pl.pallas_call(
        functools.partial(_kvprep_kernel, np_tiles=np_tiles),
        grid=(N_TOK // TM,),
        in_specs=[
            pl.BlockSpec((TM, MLA_RANK), lambda i: (i, 3072 // MLA_RANK)),
            pl.BlockSpec((TM, 128), lambda i: (i, 5632 // 128)),
            pl.BlockSpec((1, MLA_RANK), lambda i: (0, 0)),
        ],
        out_specs=[
            pl.BlockSpec((TM, MLA_RANK), lambda i: (i, 0)),
            pl.BlockSpec((TM, MLA_ROPE), lambda i: (i, 0)),
            pl.BlockSpec((TM, 128), lambda i: (i, 0)),
            pl.BlockSpec((TM, 128), lambda i: (i, 0)),
        ],
        out_shape=[
            jax.ShapeDtypeStruct((N_TOK, MLA_RANK), F32),
            jax.ShapeDtypeStruct((N_TOK, MLA_ROPE), F32),
            jax.ShapeDtypeStruct((N_TOK, 128), F32),
            jax.ShapeDtypeStruct((N_TOK, 128), F32),
        ],
        compiler_params=_params(("arbitrary",)),
        name="mla_kvprep",
    )(proj, proj, kv_norm_g.reshape(1, MLA_RANK))


def _kvup_kernel(ckv_ref, kr_ref, wuk_ref, wuv_ref, k_ref, v_ref):
    c = ckv_ref[...].astype(BF16)
    kn = _dot(c, wuk_ref[...]).astype(BF16)
    vv = _dot(c, wuv_ref[...]).astype(BF16)
    kr = kr_ref[...].astype(BF16)
    for h in range(MLA_H):
        k_ref[h, :, :MLA_NOPE] = kn[:, h * MLA_NOPE:(h + 1) * MLA_NOPE]
        k_ref[h, :, MLA_NOPE:] = kr
        v_ref[h] = vv[:, h * MLA_VH:(h + 1) * MLA_VH]


def mla_kvup(ckv_n, kr_r, w_uk2, w_uv2):
    return pl.pallas_call(
        _kvup_kernel,
        grid=(NP_TOK // TM,),
        in_specs=[
            pl.BlockSpec((TM, MLA_RANK), lambda i: (i, 0)),
            pl.BlockSpec((TM, MLA_ROPE), lambda i: (i, 0)),
            pl.BlockSpec((MLA_RANK, MLA_H * MLA_NOPE), lambda i: (0, 0)),
            pl.BlockSpec((MLA_RANK, MLA_H * MLA_VH), lambda i: (0, 0)),
        ],
        out_specs=[
            pl.BlockSpec((MLA_H, TM, MLA_QK), lambda i: (0, i, 0)),
            pl.BlockSpec((MLA_H, TM, MLA_VH), lambda i: (0, i, 0)),
        ],
        out_shape=[
            jax.ShapeDtypeStruct((MLA_H, NP_TOK, MLA_QK), BF16),
            jax.ShapeDtypeStruct((MLA_H, NP_TOK, MLA_VH), BF16),
        ],
        compiler_params=_params(("arbitrary",)),
        name="mla_kvup",
    )(ckv_n, kr_r, w_uk2, w_uv2)


FLASH_T = 512


def _rope_pair_select(x2, h):
    return jnp.where(h % 2 == 1, pltpu.roll(x2, MLA_ROPE, 1), x2)[:, :MLA_ROPE]


FLASH_G = 4


def _lanes(x, n):
    return x if n == 128 else jnp.concatenate([x] * (n // 128), axis=1)


def _flash_kernel(qt_ref, kt_ref, qn_ref, qr_ref, c4_ref, s4_ref, k_ref, v_ref, gate_ref, o_ref,
                  q_sc, m_sc, l_sc, acc_sc):
    s_id = pl.program_id(2)
    qi = qt_ref[s_id]
    ki = kt_ref[s_id]
    T, G = FLASH_T, FLASH_G

    @pl.when(ki == 0)
    def _():
        x = qr_ref[...]
        c = _lanes(c4_ref[...], G * MLA_ROPE)
        s = _lanes(s4_ref[...], G * MLA_ROPE)
        rot = (x * c + _rope_partner(x) * s) * MLA_SCALE
        for g in range(G):
            q_sc[g, :, :MLA_NOPE] = (qn_ref[:, g * MLA_NOPE:(g + 1) * MLA_NOPE] * MLA_SCALE).astype(BF16)
            q_sc[g, :, MLA_NOPE:] = rot[:, g * MLA_ROPE:(g + 1) * MLA_ROPE].astype(BF16)
        m_sc[...] = jnp.full_like(m_sc, _NEG)
        l_sc[...] = jnp.zeros_like(l_sc)
        acc_sc[...] = jnp.zeros_like(acc_sc)

    def step(masked):
        for g in range(G):
            s = _dot_nt(q_sc[g], k_ref[g])
            if masked:
                r = lax.broadcasted_iota(jnp.int32, (T, T), 0)
                c = lax.broadcasted_iota(jnp.int32, (T, T), 1)
                s = jnp.where(c <= r, s, _NEG)
            m_prev = m_sc[g]
            m_new = jnp.maximum(m_prev, jnp.max(s, axis=-1, keepdims=True))
            alpha = jnp.exp(m_prev - m_new)
            p = jnp.exp(s - _lanes(m_new, T))
            l_sc[g] = alpha * l_sc[g] + jnp.sum(p, axis=-1, keepdims=True)
            acc_sc[g] = alpha * acc_sc[g] + _dot(p.astype(BF16), v_ref[g])
            m_sc[g] = m_new

    @pl.when(ki < qi)
    def _():
        step(False)

    @pl.when(ki == qi)
    def _():
        step(True)
        for g in range(G):
            sl = slice(g * MLA_VH, (g + 1) * MLA_VH)
            o = acc_sc[g] / l_sc[g]
            o_ref[:, sl] = (o * _silu(gate_ref[:, sl])).astype(BF16)


def mla_flash(proj, c4, s4, kcat, vv):
    T, G = FLASH_T, FLASH_G
    nq = SEQ // T
    pairs = [(qi, ki) for qi in range(nq) for ki in range(qi + 1)]
    qt = jnp.asarray(np.array([p[0] for p in pairs], np.int32))
    kt = jnp.asarray(np.array([p[1] for p in pairs], np.int32))

    def qrow(b, g, s, qt, kt):
        return b * nq + qt[s]

    def krow(b, g, s, qt, kt):
        return b * nq + kt[s]

    grid_spec = pltpu.PrefetchScalarGridSpec(
        num_scalar_prefetch=2,
        grid=(BATCH, MLA_H // G, len(pairs)),
        in_specs=[
            pl.BlockSpec((T, G * MLA_NOPE), lambda b, g, s, qt, kt: (qrow(b, g, s, qt, kt), g)),
            pl.BlockSpec((T, G * MLA_ROPE),
                         lambda b, g, s, qt, kt: (qrow(b, g, s, qt, kt), 2048 // (G * MLA_ROPE) + g)),
            pl.BlockSpec((T, 128), lambda b, g, s, qt, kt: (qrow(b, g, s, qt, kt), 0)),
            pl.BlockSpec((T, 128), lambda b, g, s, qt, kt: (qrow(b, g, s, qt, kt), 0)),
            pl.BlockSpec((G, T, MLA_QK), lambda b, g, s, qt, kt: (g, krow(b, g, s, qt, kt), 0)),
            pl.BlockSpec((G, T, MLA_VH), lambda b, g, s, qt, kt: (g, krow(b, g, s, qt, kt), 0)),
            pl.BlockSpec((T, G * MLA_VH),
                         lambda b, g, s, qt, kt: (qrow(b, g, s, qt, kt), 3584 // (G * MLA_VH) + g)),
        ],
        out_specs=pl.BlockSpec((T, G * MLA_VH), lambda b, g, s, qt, kt: (qrow(b, g, s, qt, kt), g)),
        scratch_shapes=[
            pltpu.VMEM((G, T, MLA_QK), BF16),
            pltpu.VMEM((G, T, 128), F32),
            pltpu.VMEM((G, T, 128), F32),
            pltpu.VMEM((G, T, MLA_VH), F32),
        ],
    )
    return pl.pallas_call(
        _flash_kernel,
        grid_spec=grid_spec,
        out_shape=jax.ShapeDtypeStruct((NP_TOK, MLA_H * MLA_VH), BF16),
        compiler_params=_params(("arbitrary", "arbitrary", "arbitrary")),
        name="mla_flash",
    )(qt, kt, proj, proj, c4, s4, kcat, vv, proj)


def _qabs_kernel(qn_ref, qr_ref, c4_ref, s4_ref, wukt_ref, o_ref):
    h = pl.program_id(0)
    ql = _dot(qn_ref[...].astype(BF16), wukt_ref[0]) * MLA_SCALE
    x2 = qr_ref[...]
    rot = x2 * c4_ref[...] + _rope_partner(x2) * s4_ref[...]
    qr = _rope_pair_select(rot, h) * MLA_SCALE
    o_ref[:, 0, :, :MLA_RANK] = ql.reshape(DEC_BATCH, DEC_SEQ, MLA_RANK)
    o_ref[:, 0, :, MLA_RANK:] = qr.reshape(DEC_BATCH, DEC_SEQ, MLA_ROPE)


def mla_qabs(proj, c4, s4, w_ukt3):
    rb = NP_TOK // NS_TOK
    return pl.pallas_call(
        _qabs_kernel,
        grid=(MLA_H,),
        in_specs=[
            pl.BlockSpec((NS_TOK, 128), lambda h: (rb, h)),
            pl.BlockSpec((NS_TOK, 128), lambda h: (rb, 2048 // 128 + h // 2)),
            pl.BlockSpec((NS_TOK, 128), lambda h: (rb, 0)),
            pl.BlockSpec((NS_TOK, 128), lambda h: (rb, 0)),
            pl.BlockSpec((1, MLA_NOPE, MLA_RANK), lambda h: (h, 0, 0)),
        ],
        out_specs=pl.BlockSpec((DEC_BATCH, 1, DEC_SEQ, MLA_LAT), lambda h: (0, h, 0, 0)),
        out_shape=jax.ShapeDtypeStruct((DEC_BATCH, MLA_H, DEC_SEQ, MLA_LAT), F32),
        compiler_params=_params(("arbitrary",)),
        name="mla_qabs",
    )(proj, proj, c4, s4, w_ukt3)


DEC_SEQS = 2
DEC_PG = 8
DEC_NG = N_PAGES // DEC_PG
DEC_SLOTS = 3
DEC_AHEAD = DEC_SLOTS - 1


def _decode_kernel(pt_ref, q_ref, cnew_ref, knew_ref, ckv_hbm, krt_hbm, o_ref,
                   kv_buf, kr_buf, sem, m_sc, l_sc, acc_sc, *, layer):
    step = pl.program_id(0)
    n_groups = pl.num_programs(0) * DEC_NG
    R = MLA_H * DEC_SEQ

    def group_copies(n, slot):
        first_page = lax.div(n, DEC_NG) * (DEC_SEQS * N_PAGES) + lax.rem(n, DEC_NG) * DEC_PG
        cps = []
        for s in range(DEC_SEQS):
            for r in range(DEC_PG):
                page = pt_ref[first_page + s * N_PAGES + r]
                j = s * DEC_PG + r
                cps.append(pltpu.make_async_copy(ckv_hbm.at[layer, page], kv_buf.at[slot, j], sem.at[0, slot]))
                cps.append(pltpu.make_async_copy(krt_hbm.at[layer, page], kr_buf.at[slot, j], sem.at[1, slot]))
        return cps

    def start_group(n, slot):
        for i, cp in enumerate(group_copies(n, slot)):
            cp.start(priority=(i // 2) % 2)

    @pl.when(step == 0)
    def _():
        for n0 in range(DEC_AHEAD):
            start_group(n0, n0)

    m_sc[...] = jnp.full_like(m_sc, _NEG)
    l_sc[...] = jnp.zeros_like(l_sc)
    acc_sc[...] = jnp.zeros_like(acc_sc)

    ql, qr = [], []
    for s in range(DEC_SEQS):
        q = q_ref[s].reshape(R, MLA_LAT)
        ql.append(q[:, :MLA_RANK].astype(BF16))
        qr.append(q[:, MLA_RANK:].astype(BF16))

    def online(s, scores, vals):
        m_prev = m_sc[s]
        m_new = jnp.maximum(m_prev, jnp.max(scores, axis=-1, keepdims=True))
        alpha = jnp.exp(m_prev - m_new)
        p = jnp.exp(scores - _lanes(m_new, scores.shape[1]))
        l_sc[s] = alpha * l_sc[s] + jnp.sum(p, axis=-1, keepdims=True)
        acc_sc[s] = _lanes(alpha, MLA_RANK) * acc_sc[s] + _dot(p.astype(BF16), vals)
        m_sc[s] = m_new

    def group(g, carry):
        n = step * DEC_NG + g
        slot = lax.rem(n, DEC_SLOTS)
        for cp in group_copies(n, slot):
            cp.wait()

        @pl.when(n + DEC_AHEAD < n_groups)
        def _():
            start_group(n + DEC_AHEAD, lax.rem(n + DEC_AHEAD, DEC_SLOTS))

        for s in range(DEC_SEQS):
            kv = jnp.concatenate(
                [kv_buf[slot, s * DEC_PG + r].astype(BF16) for r in range(DEC_PG)], axis=0)
            krt = jnp.concatenate(
                [kr_buf[slot, s * DEC_PG + r].astype(BF16) for r in range(DEC_PG)], axis=1)
            online(s, _dot_nt(ql[s], kv) + _dot(qr[s], krt), kv)
        return carry

    lax.fori_loop(0, DEC_NG, group, 0)

    t = lax.broadcasted_iota(jnp.int32, (R, PAGE), 0) & (DEC_SEQ - 1)
    j = lax.broadcasted_iota(jnp.int32, (R, PAGE), 1)
    for s in range(DEC_SEQS):
        rows = slice(s * DEC_SEQ, (s + 1) * DEC_SEQ)
        cn = jnp.concatenate([cnew_ref[rows, :], jnp.zeros((PAGE - DEC_SEQ, MLA_RANK), F32)], axis=0).astype(BF16)
        kn = jnp.concatenate([knew_ref[rows, :], jnp.zeros((PAGE - DEC_SEQ, MLA_ROPE), F32)], axis=0).astype(BF16)
        sc = _dot_nt(ql[s], cn) + _dot_nt(qr[s], kn)
        online(s, jnp.where(j <= t, sc, _NEG), cn)
        o = acc_sc[s] / _lanes(l_sc[s], MLA_RANK)
        o_ref[s] = o.reshape(MLA_H, DEC_SEQ, MLA_RANK)


def mla_decode(qcat, cache_ckv, cache_krt, layer, page_table, ckv_n, kr_r):
    rows = DEC_SEQS * DEC_SEQ
    r0 = NP_TOK // rows
    n_pg = DEC_SEQS * DEC_PG
    R = MLA_H * DEC_SEQ
    grid_spec = pltpu.PrefetchScalarGridSpec(
        num_scalar_prefetch=1,
        grid=(DEC_BATCH // DEC_SEQS,),
        in_specs=[
            pl.BlockSpec((DEC_SEQS, MLA_H, DEC_SEQ, MLA_LAT), lambda i, pt: (i, 0, 0, 0)),
            pl.BlockSpec((rows, MLA_RANK), lambda i, pt: (r0 + i, 0)),
            pl.BlockSpec((rows, MLA_ROPE), lambda i, pt: (r0 + i, 0)),
            pl.BlockSpec(memory_space=pl.ANY),
            pl.BlockSpec(memory_space=pl.ANY),
        ],
        out_specs=pl.BlockSpec((DEC_SEQS, MLA_H, DEC_SEQ, MLA_RANK), lambda i, pt: (i, 0, 0, 0)),
        scratch_shapes=[
            pltpu.VMEM((DEC_SLOTS, n_pg, PAGE, MLA_RANK), F32),
            pltpu.VMEM((DEC_SLOTS, n_pg, MLA_ROPE, PAGE), F32),
            pltpu.SemaphoreType.DMA((2, DEC_SLOTS)),
            pltpu.VMEM((DEC_SEQS, R, 128), F32),
            pltpu.VMEM((DEC_SEQS, R, 128), F32),
            pltpu.VMEM((DEC_SEQS, R, MLA_RANK), F32),
        ],
    )
    return pl.pallas_call(
        functools.partial(_decode_kernel, layer=layer),
        grid_spec=grid_spec,
        out_shape=jax.ShapeDtypeStruct((DEC_BATCH, MLA_H, DEC_SEQ, MLA_RANK), F32),
        compiler_params=_params(("arbitrary",)),
        name="mla_decode",
    )(page_table.reshape(-1), qcat, ckv_n, kr_r, cache_ckv, cache_krt)


def _uvup_kernel(ol_ref, wuv_ref, gate_ref, a_ref):
    ol = ol_ref[...].reshape(NS_TOK, MLA_RANK).astype(BF16)
    a_ref[...] = (_dot(ol, wuv_ref[0]) * _silu(gate_ref[...])).astype(BF16)


def mla_uvup(o_lat, w_uv3, proj):
    rb = NP_TOK // NS_TOK
    return pl.pallas_call(
        _uvup_kernel,
        grid=(MLA_H,),
        in_specs=[
            pl.BlockSpec((DEC_BATCH, 1, DEC_SEQ, MLA_RANK), lambda h: (0, h, 0, 0)),
            pl.BlockSpec((1, MLA_RANK, MLA_VH), lambda h: (h, 0, 0)),
            pl.BlockSpec((NS_TOK, 128), lambda h: (rb, 3584 // 128 + h)),
        ],
        out_specs=pl.BlockSpec((NS_TOK, MLA_VH), lambda h: (0, h)),
        out_shape=jax.ShapeDtypeStruct((NS_TOK, MLA_H * MLA_VH), BF16),
        compiler_params=_params(("arbitrary",)),
        name="mla_uvup",
    )(o_lat, w_uv3, proj)


POOL_HALO = 16


def _pool_prompt_kernel(u_ref, halo_ref, gate_ref, wg_ref, sc_ref, a_ref, tail_ref, ext_sc, *, tiles_per_seq):
    i = pl.program_id(0)
    g = pl.program_id(1)
    first = (i % tiles_per_seq) == 0
    u = u_ref[...]
    tail_ref[0] = u[TM - POOL_HALO:, :]
    ext_sc[:POOL_HALO, :] = jnp.where(first, 0.0, halo_ref[...])
    ext_sc[POOL_HALO:, :] = u
    t = (i % tiles_per_seq) * TM + lax.broadcasted_iota(jnp.int32, (TM, 1), 0)

    for gi, w in enumerate(POOL_WINDOWS):
        @pl.when(g == gi)
        def _(w=w):
            acc = u
            for j in range(1, w):
                acc = acc + ext_sc[POOL_HALO - j:POOL_HALO - j + TM, :]
            cnt = jnp.minimum(t + 1, w).astype(F32)
            p = (acc / cnt - u).astype(BF16)
            z = _dot(p, wg_ref[0]) * sc_ref[...]
            a_ref[...] = (z * _silu(gate_ref[...])).astype(BF16)


def pool_prompt(proj, w_grp, pscale):
    tiles_per_seq = SEQ // TM
    hb = TM // POOL_HALO
    return pl.pallas_call(
        functools.partial(_pool_prompt_kernel, tiles_per_seq=tiles_per_seq),
        grid=(NP_TOK // TM, len(POOL_WINDOWS)),
        in_specs=[
            pl.BlockSpec((TM, POOL_G), lambda i, g: (i, g)),
            pl.BlockSpec((POOL_HALO, POOL_G), lambda i, g: (jnp.maximum(i * hb - 1, 0), g)),
            pl.BlockSpec((TM, POOL_G), lambda i, g: (i, len(POOL_WINDOWS) + g)),
            pl.BlockSpec((1, POOL_G, POOL_G), lambda i, g: (g, 0, 0)),
            pl.BlockSpec((1, POOL_G), lambda i, g: (0, g)),
        ],
        out_specs=[
            pl.BlockSpec((TM, POOL_G), lambda i, g: (i, g)),
            pl.BlockSpec((1, POOL_HALO, POOL_G), lambda i, g: (i, 0, g)),
        ],
        out_shape=[
            jax.ShapeDtypeStruct((NP_TOK, D), BF16),
            jax.ShapeDtypeStruct((NP_TOK // TM, POOL_HALO, D), F32),
        ],
        scratch_shapes=[pltpu.VMEM((POOL_HALO + TM, POOL_G), F32)],
        compiler_params=_params(("arbitrary", "arbitrary")),
        name="pool_prompt",
    )(proj, proj, proj, w_grp, pscale)


def _pool_sample_kernel(u_ref, gate_ref, hist_ref, wg_ref, sc_ref, a_ref):
    g = pl.program_id(0)

    def seq(r):
        return hist_ref[r] if r < POOL_HIST else u_ref[r - POOL_HIST]

    for gi, w in enumerate(POOL_WINDOWS):
        @pl.when(g == gi)
        def _(w=w):
            ps = []
            for t in range(DEC_SEQ):
                acc = seq(POOL_HIST + t)
                for j in range(1, w):
                    acc = acc + seq(POOL_HIST + t - j)
                ps.append(acc / float(w) - u_ref[t])
            p = jnp.concatenate(ps, axis=0).astype(BF16)
            z = _dot(p, wg_ref[0]) * sc_ref[...]
            gate = gate_ref[...].reshape(NS_TOK, POOL_G)
            a_ref[...] = (z * _silu(gate)).astype(BF16).reshape(DEC_SEQ, DEC_BATCH, POOL_G)


def pool_sample(proj_t, hist_t, w_grp, pscale):
    ng = len(POOL_WINDOWS)
    return pl.pallas_call(
        _pool_sample_kernel,
        grid=(ng,),
        in_specs=[
            pl.BlockSpec((DEC_SEQ, DEC_BATCH, POOL_G), lambda g: (0, 0, g)),
            pl.BlockSpec((DEC_SEQ, DEC_BATCH, POOL_G), lambda g: (0, 0, ng + g)),
            pl.BlockSpec((POOL_HIST, DEC_BATCH, POOL_G), lambda g: (0, 0, g)),
            pl.BlockSpec((1, POOL_G, POOL_G), lambda g: (g, 0, 0)),
            pl.BlockSpec((1, POOL_G), lambda g: (0, g)),
        ],
        out_specs=pl.BlockSpec((DEC_SEQ, DEC_BATCH, POOL_G), lambda g: (0, 0, g)),
        out_shape=jax.ShapeDtypeStruct((DEC_SEQ, DEC_BATCH, D), BF16),
        compiler_params=_params(("arbitrary",)),
        name="pool_sample",
    )(proj_t, proj_t, hist_t, w_grp, pscale)


def kernel(x_prompt, x_sample, c_prompt, c_sample, state_gla, cache_ckv, cache_kr, state_pool, page_table, norm_g, ada_w, ada_b, final_norm_g, gla_w_in, gla_w_gate_up, gla_b_gate, gla_onorm_g, gla_w_out, mla_w_in, mla_kv_norm_g, mla_w_uk, mla_w_uv, mla_w_out, pool_w_in, pool_w_grp, pool_scale, pool_w_out):
    x_all = jnp.concatenate([x_prompt.reshape(NP_TOK, D), x_sample.reshape(NS_TOK, D)], axis=0)

    n_c = BATCH + DEC_BATCH
    c_all = jnp.concatenate([c_prompt, c_sample, jnp.zeros((8 - n_c % 8, D), F32)], axis=0)
    mod = ada_mod(c_all, ada_w, ada_b)

    gla_states_p, gla_states_s = [], None
    ckv_rows, kr_rows, pool_p, pool_s = [], [], [], []
    ia = ib = ic = 0
    mods = [(mod[l, :BATCH].reshape(BATCH, 1, 3 * D), jnp.repeat(mod[l, BATCH:n_c], DEC_SEQ, axis=0))
            for l in range(DEPTH)]
    h = norm_mod(x_all, norm_g[0], *mods[0])
    for l in range(DEPTH):
        mod_p, mod_s = mods[l]
        mixer = LAYER_MIXER[l]
        if mixer == 0:
            w_t = jnp.swapaxes(gla_w_in[ia], 0, 1)
            cut = 2 * GLA_HK + GLA_HV
            w_main = jnp.concatenate([w_t[:cut], w_t[cut + GLA_RANK:]], axis=0).astype(BF16)
            w_glr = jnp.pad(w_t[cut:cut + GLA_RANK], ((0, 128 - GLA_RANK), (0, 0))).astype(BF16)
            w_up = jnp.pad(gla_w_gate_up[ia], ((0, 128 - GLA_RANK), (0, 0))).astype(BF16)
            proj = proj_matmul(h, w_main, 1024, True)
            gk = gla_gate(h, w_glr, w_up, gla_b_gate[ia].reshape(1, GLA_HK))
            a_p, st_p = gla_prompt(proj, gk, gla_onorm_g[ia])
            a_s, gla_states_s = gla_sample(proj, gk, gla_onorm_g[ia], state_gla, ia, states_out=gla_states_s)
            gla_states_p.append(st_p)
            w_out = gla_w_out[ia].astype(BF16)
            ia += 1
        elif mixer == 1:
            w_t = jnp.swapaxes(mla_w_in[ib], 0, 1)
            nq = MLA_H * MLA_QK
            wq = w_t[:nq].reshape(MLA_H, MLA_QK, D)
            w_perm = jnp.concatenate([
                wq[:, :MLA_NOPE].reshape(MLA_H * MLA_NOPE, D),
                wq[:, MLA_NOPE:].reshape(MLA_H * MLA_ROPE, D),
                w_t[nq:nq + MLA_RANK],
                w_t[nq + MLA_RANK + MLA_ROPE:],
                w_t[nq + MLA_RANK:nq + MLA_RANK + MLA_ROPE],
                jnp.zeros((128 - MLA_ROPE, D), F32),
            ], axis=0).astype(BF16)
            proj = proj_matmul(h, w_perm, 1152, True)
            ckv_n, kr_r, c4, s4 = mla_kvprep(proj, mla_kv_norm_g[ib])
            w_uk2 = mla_w_uk[ib].reshape(MLA_RANK, MLA_H * MLA_NOPE).astype(BF16)
            w_uv2 = mla_w_uv[ib].reshape(MLA_RANK, MLA_H * MLA_VH).astype(BF16)
            w_ukt3 = jnp.transpose(mla_w_uk[ib], (1, 2, 0)).astype(BF16)
            w_uv3 = jnp.transpose(mla_w_uv[ib], (1, 0, 2)).astype(BF16)
            kcat, vv = mla_kvup(ckv_n, kr_r, w_uk2, w_uv2)
            a_p = mla_flash(proj, c4, s4, kcat, vv)
            qcat = mla_qabs(proj, c4, s4, w_ukt3)
            cache_krt = jnp.swapaxes(cache_kr, 2, 3)
            o_lat = mla_decode(qcat, cache_ckv, cache_krt, ib, page_table, ckv_n, kr_r)
            a_s = mla_uvup(o_lat, w_uv3, proj)
            ckv_rows.append(ckv_n)
            kr_rows.append(kr_r)
            w_out = mla_w_out[ib].astype(BF16)
            ib += 1
        else:
            proj = proj_matmul(h, pool_w_in[ic].astype(BF16), 1024, False)
            w_grp = pool_w_grp[ic].astype(BF16)
            pscale = pool_scale[ic].reshape(1, D)
            a_p, u_tails = pool_prompt(proj, w_grp, pscale)
            proj_t = jnp.transpose(proj[NP_TOK:].reshape(DEC_BATCH, DEC_SEQ, 2 * D), (1, 0, 2))
            hist_t = jnp.transpose(state_pool[ic], (1, 0, 2))
            a_st = pool_sample(proj_t, hist_t, w_grp, pscale)
            a_s = jnp.transpose(a_st, (1, 0, 2)).reshape(NS_TOK, D)
            u_s = proj[NP_TOK:, :D].reshape(DEC_BATCH, DEC_SEQ, D)
            pool_p.append(u_tails.reshape(BATCH, SEQ // TM, POOL_HALO, D)[:, -1, POOL_HALO - POOL_HIST:, :])
            pool_s.append(jnp.concatenate([state_pool[ic][:, DEC_SEQ:, :], u_s], axis=1))
            w_out = pool_w_out[ic].astype(BF16)
            ic += 1
        if l == DEPTH - 1:
            y_p, y_s = out_proj(a_p, a_s, w_out, x_all, mod_p, mod_s, final_g=final_norm_g)
        else:
            x_all, h = out_proj(a_p, a_s, w_out, x_all, mod_p, mod_s, next_norm=(norm_g[l + 1],) + mods[l + 1])

    y_prompt = y_p.reshape(BATCH, SEQ, D)
    y_sample = y_s.reshape(DEC_BATCH, DEC_SEQ, D)
    ckv_all = jnp.stack(ckv_rows)
    kr_all = jnp.stack(kr_rows)
    return (
        y_prompt,
        y_sample,
        jnp.stack(gla_states_p),
        gla_states_s,
        ckv_all[:, :NP_TOK].reshape(-1, BATCH, SEQ, MLA_RANK),
        kr_all[:, :NP_TOK].reshape(-1, BATCH, SEQ, MLA_ROPE),
        ckv_all[:, NP_TOK:].reshape(-1, DEC_BATCH, DEC_SEQ, MLA_RANK),
        kr_all[:, NP_TOK:].reshape(-1, DEC_BATCH, DEC_SEQ, MLA_ROPE),
        jnp.stack(pool_p),
        jnp.stack(pool_s),
    )
```

```python
import functools

import numpy as np
import jax
import jax.numpy as jnp
from jax import lax
from jax.experimental import pallas as pl
from jax.experimental.pallas import tpu as pltpu

F32 = jnp.float32
BF16 = jnp.bfloat16

D = 2048
BATCH = 4
SEQ = 2048
DEC_BATCH = 128
DEC_SEQ = 8
PAGE = 128
N_PAGES = 64
PAST = N_PAGES * PAGE
DEPTH = 4
LAYER_MIXER = (0, 1, 2, 0)
EPS = 1e-6

NP_TOK = BATCH * SEQ
NS_TOK = DEC_BATCH * DEC_SEQ
N_TOK = NP_TOK + NS_TOK

GLA_H = 4
GLA_DK = 256
GLA_DV = 512
GLA_RANK = 16
GLA_TAU = 16.0
GLA_HK = GLA_H * GLA_DK
GLA_HV = GLA_H * GLA_DV
GLA_C = 128
GLA_SUB = 16

MLA_H = 16
MLA_NOPE = 128
MLA_ROPE = 64
MLA_VH = 128
MLA_RANK = 512
MLA_SCALE = (MLA_NOPE + MLA_ROPE) ** -0.5
MLA_QK = MLA_NOPE + MLA_ROPE
MLA_LAT = MLA_RANK + MLA_ROPE
MLA_N = 5760
ROPE_BASE = 10000.0

POOL_WINDOWS = (2, 4, 8, 16)
POOL_G = 512
POOL_HIST = 15

TM = 512
TM_OUT = 256
VMEM_LIMIT = 56 * 1024 * 1024

_NEG = -1e30


def _params(sem):
    return pltpu.CompilerParams(dimension_semantics=sem, vmem_limit_bytes=VMEM_LIMIT)


def _silu(x):
    return x * (1.0 / (1.0 + jnp.exp(-x)))


def _dot(a, b):
    return jnp.dot(a, b, preferred_element_type=F32)


def _dot_nt(a, b):
    return lax.dot_general(a, b, (((1,), (1,)), ((), ())), preferred_element_type=F32)


def _ada_kernel(c_ref, w_ref, b_ref, o_ref):
    sc = _silu(c_ref[...]).astype(BF16)
    o_ref[0] = _dot(sc, w_ref[0].astype(BF16)) + b_ref[0]


def ada_mod(c_all, ada_w, ada_b):
    rows = c_all.shape[0]
    tn = 768
    return pl.pallas_call(
        _ada_kernel,
        grid=(DEPTH, 3 * D // tn),
        in_specs=[
            pl.BlockSpec((rows, D), lambda l, j: (0, 0)),
            pl.BlockSpec((1, D, tn), lambda l, j: (l, 0, j)),
            pl.BlockSpec((1, 1, tn), lambda l, j: (l, 0, j)),
        ],
        out_specs=pl.BlockSpec((1, rows, tn), lambda l, j: (l, 0, j)),
        out_shape=jax.ShapeDtypeStruct((DEPTH, rows, 3 * D), F32),
        compiler_params=_params(("arbitrary", "arbitrary")),
        name="ada_mod",
    )(c_all, ada_w, ada_b.reshape(DEPTH, 1, 3 * D))


def _mod_rows(i, np_tiles, p_ref, s_ref):
    return jnp.where(i < np_tiles, p_ref[0], s_ref[...])


def _norm_mod(x, g, shift, scale):
    ms = jnp.mean(x * x, axis=-1, keepdims=True)
    return ((x * lax.rsqrt(ms + EPS) * g) * (1.0 + scale) + shift).astype(BF16)


def _mod_specs(tm, cols):
    np_tiles = NP_TOK // tm
    tiles_per_seq = SEQ // tm
    specs = []
    for col in cols:
        specs.append(pl.BlockSpec((1, 1, D), lambda i, col=col: (jnp.minimum(i // tiles_per_seq, BATCH - 1), 0, col)))
        specs.append(pl.BlockSpec((tm, D), lambda i, col=col: (jnp.maximum(i - np_tiles, 0), col)))
    return specs


def _stream_args(tm, x):
    np_tiles = NP_TOK // tm
    xp, xs = x if isinstance(x, tuple) else (x, x)
    s_off = 0 if isinstance(x, tuple) else np_tiles
    specs = [
        pl.BlockSpec((tm, D), lambda i: (jnp.minimum(i, np_tiles - 1), 0)),
        pl.BlockSpec((tm, D), lambda i: (jnp.maximum(i - np_tiles, 0) + s_off, 0)),
    ]
    return specs, [xp, xs]


def _norm_kernel(xp_ref, xs_ref, g_ref, shp_ref, shs_ref, scp_ref, scs_ref, h_ref, *, np_tiles):
    i = pl.program_id(0)
    x = jnp.where(i < np_tiles, xp_ref[...], xs_ref[...])
    h_ref[...] = _norm_mod(x, g_ref[...], _mod_rows(i, np_tiles, shp_ref, shs_ref),
                           _mod_rows(i, np_tiles, scp_ref, scs_ref))


def norm_mod(x, norm_g, mod_p, mod_s):
    tm = TM
    x_specs, x_args = _stream_args(tm, x)
    return pl.pallas_call(
        functools.partial(_norm_kernel, np_tiles=NP_TOK // tm),
        grid=(N_TOK // tm,),
        in_specs=x_specs + [pl.BlockSpec((1, D), lambda i: (0, 0))] + _mod_specs(tm, (0, 1)),
        out_specs=pl.BlockSpec((tm, D), lambda i: (i, 0)),
        out_shape=jax.ShapeDtypeStruct((N_TOK, D), BF16),
        compiler_params=_params(("arbitrary",)),
        name="norm_mod",
    )(*x_args, norm_g.reshape(1, D), mod_p, mod_s, mod_p, mod_s)


TMM = 1024


def _mm_kernel(h_ref, w_ref, o_ref, *, w_is_nk):
    o_ref[...] = _dot_nt(h_ref[...], w_ref[...]) if w_is_nk else _dot(h_ref[...], w_ref[...])


def proj_matmul(h, w, tn, w_is_nk, n=None):
    if n is None:
        n = w.shape[0] if w_is_nk else w.shape[1]
    w_spec = pl.BlockSpec((tn, D), lambda j, i: (j, 0)) if w_is_nk else pl.BlockSpec((D, tn), lambda j, i: (0, j))
    return pl.pallas_call(
        functools.partial(_mm_kernel, w_is_nk=w_is_nk),
        grid=(n // tn, N_TOK // TMM),
        in_specs=[pl.BlockSpec((TMM, D), lambda j, i: (i, 0)), w_spec],
        out_specs=pl.BlockSpec((TMM, tn), lambda j, i: (i, j)),
        out_shape=jax.ShapeDtypeStruct((N_TOK, n), F32),
        compiler_params=_params(("arbitrary", "arbitrary")),
        name="proj_matmul",
    )(h, w)


def _gate_kernel(h_ref, wg_ref, wu_ref, bg_ref, gk_ref):
    glr = _dot_nt(h_ref[...], wg_ref[...]).astype(BF16)
    z = _dot(glr, wu_ref[...]) + bg_ref[...]
    gk_ref[...] = (jnp.minimum(z, 0.0) - jnp.log1p(jnp.exp(-jnp.abs(z)))) / GLA_TAU


def gla_gate(h, wg, wu, bg):
    tm = TM
    return pl.pallas_call(
        _gate_kernel,
        grid=(N_TOK // tm,),
        in_specs=[
            pl.BlockSpec((tm, D), lambda i: (i, 0)),
            pl.BlockSpec(wg.shape, lambda i: (0, 0)),
            pl.BlockSpec(wu.shape, lambda i: (0, 0)),
            pl.BlockSpec(bg.shape, lambda i: (0, 0)),
        ],
        out_specs=pl.BlockSpec((tm, GLA_HK), lambda i: (i, 0)),
        out_shape=jax.ShapeDtypeStruct((N_TOK, GLA_HK), F32),
        compiler_params=_params(("arbitrary",)),
        name="gla_gate",
    )(h, wg, wu, bg)


def _out_kernel(ap_ref, as_ref, w_ref, xp_ref, xs_ref, gp_ref, gs_ref, *rest, np_tiles, final):
    i = pl.program_id(0)
    a = jnp.where(i < np_tiles, ap_ref[...], as_ref[...])
    y = _dot(a, w_ref[...])
    gate = _mod_rows(i, np_tiles, gp_ref, gs_ref)
    xn = jnp.where(i < np_tiles, xp_ref[...], xs_ref[...]) + gate * y
    if not final:
        ng_ref, shp_ref, shs_ref, scp_ref, scs_ref, o_ref, h_ref = rest
        o_ref[...] = xn
        h_ref[...] = _norm_mod(xn, ng_ref[...], _mod_rows(i, np_tiles, shp_ref, shs_ref),
                               _mod_rows(i, np_tiles, scp_ref, scs_ref))
        return
    fg_ref, yp_ref, ys_ref = rest
    ms = jnp.mean(xn * xn, axis=-1, keepdims=True)
    yn = xn * lax.rsqrt(ms + EPS) * fg_ref[...]

    @pl.when(i < np_tiles)
    def _():
        yp_ref[...] = yn

    @pl.when(i >= np_tiles)
    def _():
        ys_ref[...] = yn


def out_proj(a_p, a_s, w_out, x, mod_p, mod_s, final_g=None, next_norm=None):
    tm = TM_OUT
    np_tiles = NP_TOK // tm
    final = final_g is not None
    x_specs, x_args = _stream_args(tm, x)
    in_specs = [
        pl.BlockSpec((tm, D), lambda i: (jnp.minimum(i, np_tiles - 1), 0)),
        pl.BlockSpec((tm, D), lambda i: (jnp.maximum(i - np_tiles, 0), 0)),
        pl.BlockSpec((D, D), lambda i: (0, 0)),
    ] + x_specs + _mod_specs(tm, (2,))
    args = [a_p, a_s, w_out] + x_args + [mod_p, mod_s]
    if not final:
        ng, nmod_p, nmod_s = next_norm
        in_specs += [pl.BlockSpec((1, D), lambda i: (0, 0))] + _mod_specs(tm, (0, 1))
        args += [ng.reshape(1, D), nmod_p, nmod_s, nmod_p, nmod_s]
        out_specs = [pl.BlockSpec((tm, D), lambda i: (i, 0)), pl.BlockSpec((tm, D), lambda i: (i, 0))]
        out_shape = [jax.ShapeDtypeStruct((N_TOK, D), F32), jax.ShapeDtypeStruct((N_TOK, D), BF16)]
    else:
        in_specs.append(pl.BlockSpec((1, D), lambda i: (0, 0)))
        args.append(final_g.reshape(1, D))
        out_specs = [
            pl.BlockSpec((tm, D), lambda i: (jnp.minimum(i, np_tiles - 1), 0)),
            pl.BlockSpec((tm, D), lambda i: (jnp.maximum(i - np_tiles, 0), 0)),
        ]
        out_shape = [jax.ShapeDtypeStruct((NP_TOK, D), F32), jax.ShapeDtypeStruct((NS_TOK, D), F32)]
    return pl.pallas_call(
        functools.partial(_out_kernel, np_tiles=np_tiles, final=final),
        grid=(N_TOK // tm,),
        in_specs=in_specs,
        out_specs=out_specs,
        out_shape=out_shape,
        compiler_params=_params(("arbitrary",)),
        name="out_proj_final" if final else "out_proj",
    )(*args)


def _cumsum_rows(x):
    n = x.shape[0]
    row = lax.broadcasted_iota(jnp.int32, x.shape, 0)
    s = 1
    while s < n:
        x = x + jnp.where(row >= s, pltpu.roll(x, s, 0), 0.0)
        s *= 2
    return x


LOG2E = 1.4426950408889634


def _cumsum_mxu(x):
    n = x.shape[0]
    r = lax.broadcasted_iota(jnp.int32, (n, n), 0)
    c = lax.broadcasted_iota(jnp.int32, (n, n), 1)
    tri = jnp.where(c <= r, 1.0, 0.0).astype(BF16)
    hi = x.astype(BF16)
    rem = x - hi.astype(F32)
    mid = rem.astype(BF16)
    lo = (rem - mid.astype(F32)).astype(BF16)
    return _dot(tri, hi) + _dot(tri, mid) + _dot(tri, lo)


def _col_from_row(v):
    return jnp.broadcast_to(v, (128, v.shape[1])).T


def _head_rmsnorm_gate(o, g, og):
    ms = jnp.mean(o * o, axis=-1, keepdims=True)
    return (o * lax.rsqrt(ms + EPS) * g) * _silu(og)


def _gla_chunk(q, k, v, gk, S):
    C, SUB = GLA_C, GLA_SUB
    nsub = C // SUB

    cum = _cumsum_mxu(gk) * LOG2E
    excl = cum - gk * LOG2E
    last = cum[C - 1:C, :]

    o = _dot((q * jnp.exp2(cum)).astype(BF16), S.astype(BF16))

    row_blocks = [jnp.zeros((SUB, C), F32)]
    for i in range(1, nsub):
        sl = slice(i * SUB, (i + 1) * SUB)
        n_k = i * SUB
        b_i = excl[n_k:n_k + 1, :]
        qt = (q[sl] * jnp.exp2(cum[sl] - b_i)).astype(BF16)
        kh = (k[:n_k] * jnp.exp2(b_i - cum[:n_k])).astype(BF16)
        kh = jnp.concatenate([kh, jnp.zeros((C - n_k, GLA_DK), BF16)], axis=0)
        row_blocks.append(_dot_nt(qt, kh))
    att = jnp.concatenate(row_blocks, axis=0)

    parts = []
    for d in range(SUB):
        k_d = k if d == 0 else pltpu.roll(k, d, 0)
        cum_d = cum if d == 0 else pltpu.roll(cum, d, 0)
        parts.append((q * k_d * jnp.exp2(cum - cum_d)).astype(BF16))
    ones = jnp.ones((GLA_DK, 128), BF16)
    band = _dot(jnp.concatenate(parts, axis=0), ones)
    r = lax.broadcasted_iota(jnp.int32, (C, C), 0)
    cc = lax.broadcasted_iota(jnp.int32, (C, C), 1)
    off = jnp.where((r & -SUB) == (cc & -SUB), r - cc, -1)
    for d in range(SUB):
        att = jnp.where(off == d, band[d * C:(d + 1) * C], att)

    o = o + _dot(att.astype(BF16), v)

    kd = (k * jnp.exp2(last - cum)).astype(BF16)
    dec = _col_from_row(jnp.exp2(last))
    dec = jnp.concatenate([dec] * (GLA_DV // 128), axis=1)
    return o, dec * S + _dot(kd.T, v)


def _gla_prompt_kernel(qk_ref, v_ref, og_ref, gk_ref, g_ref, a_ref, s_ref):
    @pl.when(pl.program_id(1) == 0)
    def _():
        s_ref[...] = jnp.zeros_like(s_ref)

    for h in range(GLA_H):
        ksl = slice(h * GLA_DK, (h + 1) * GLA_DK)
        vsl = slice(h * GLA_DV, (h + 1) * GLA_DV)
        q = qk_ref[:, ksl] * (GLA_DK ** -0.5)
        k = qk_ref[:, GLA_HK + h * GLA_DK:GLA_HK + (h + 1) * GLA_DK]
        o, s_new = _gla_chunk(q, k, v_ref[:, vsl].astype(BF16), gk_ref[:, ksl], s_ref[0, h])
        s_ref[0, h] = s_new
        a_ref[:, vsl] = _head_rmsnorm_gate(o, g_ref[...], og_ref[:, vsl]).astype(BF16)


def gla_prompt(proj, og, gk, onorm_g):
    nc = SEQ // GLA_C

    def rows(b, c):
        return b * nc + c

    return pl.pallas_call(
        _gla_prompt_kernel,
        grid=(BATCH, nc),
        in_specs=[
            pl.BlockSpec((GLA_C, 2 * GLA_HK), lambda b, c: (rows(b, c), 0)),
            pl.BlockSpec((GLA_C, GLA_HV), lambda b, c: (rows(b, c), 1)),
            pl.BlockSpec((GLA_C, GLA_HV), lambda b, c: (rows(b, c), 0)),
            pl.BlockSpec((GLA_C, GLA_HK), lambda b, c: (rows(b, c), 0)),
            pl.BlockSpec((1, GLA_DV), lambda b, c: (0, 0)),
        ],
        out_specs=[
            pl.BlockSpec((GLA_C, GLA_HV), lambda b, c: (rows(b, c), 0)),
            pl.BlockSpec((1, GLA_H, GLA_DK, GLA_DV), lambda b, c: (b, 0, 0, 0)),
        ],
        out_shape=[
            jax.ShapeDtypeStruct((NP_TOK, GLA_HV), BF16),
            jax.ShapeDtypeStruct((BATCH, GLA_H, GLA_DK, GLA_DV), F32),
        ],
        compiler_params=_params(("arbitrary", "arbitrary")),
        name="gla_prompt",
    )(proj, proj, og, gk, onorm_g.reshape(1, GLA_DV))


def _gla_sample_kernel(qk_ref, v_ref, og_ref, gk_ref, g_ref, s0_ref, *rest, aliased):
    a_ref, s_ref = rest[1:] if aliased else rest
    T = DEC_SEQ
    row = lax.broadcasted_iota(jnp.int32, (T, GLA_DV), 0)
    outs = []
    for h in range(GLA_H):
        ksl = slice(h * GLA_DK, (h + 1) * GLA_DK)
        vsl = slice(h * GLA_DV, (h + 1) * GLA_DV)
        q = qk_ref[:, ksl] * (GLA_DK ** -0.5)
        k = qk_ref[:, GLA_HK + h * GLA_DK:GLA_HK + (h + 1) * GLA_DK]
        v = v_ref[:, vsl]
        gk = gk_ref[:, ksl]
        S = s0_ref[0, 0, h]

        cum = _cumsum_rows(gk)
        last = cum[T - 1:T, :]
        o = _dot((q * jnp.exp(cum)).astype(BF16), S.astype(BF16))
        for d in range(T):
            k_d = k if d == 0 else pltpu.roll(k, d, 0)
            cum_d = cum if d == 0 else pltpu.roll(cum, d, 0)
            v_d = v if d == 0 else pltpu.roll(v, d, 0)
            w = jnp.sum(q * k_d * jnp.exp(jnp.minimum(cum - cum_d, 0.0)), axis=-1, keepdims=True)
            o = o + jnp.where(row >= d, w * v_d, 0.0)

        kd = k * jnp.exp(last - cum)
        stacked = jnp.concatenate(
            [kd, jnp.broadcast_to(jnp.exp(last), (T, GLA_DK)), jnp.zeros((128 - 2 * T, GLA_DK), F32)], axis=0)
        st = stacked.T
        vpad = jnp.concatenate([v, jnp.zeros((128 - T, GLA_DV), F32)], axis=0)
        lane = lax.broadcasted_iota(jnp.int32, (GLA_DK, 128), 1)
        kdt = jnp.where(lane < T, st, 0.0).astype(BF16)
        s_ref[0, 0, h] = st[:, T:T + 1] * S + _dot(kdt, vpad.astype(BF16))

        outs.append(_head_rmsnorm_gate(o, g_ref[...], og_ref[:, vsl]))
    a_ref[...] = jnp.concatenate(outs, axis=1).astype(BF16)


def gla_sample(proj, og, gk, onorm_g, state_all, layer, states_out=None):
    r0 = NP_TOK // DEC_SEQ
    n_a = state_all.shape[0]
    st_block = (1, 1, GLA_H, GLA_DK, GLA_DV)
    in_specs = [
        pl.BlockSpec((DEC_SEQ, 2 * GLA_HK), lambda b: (r0 + b, 0)),
        pl.BlockSpec((DEC_SEQ, GLA_HV), lambda b: (r0 + b, 1)),
        pl.BlockSpec((DEC_SEQ, GLA_HV), lambda b: (r0 + b, 0)),
        pl.BlockSpec((DEC_SEQ, GLA_HK), lambda b: (r0 + b, 0)),
        pl.BlockSpec((1, GLA_DV), lambda b: (0, 0)),
        pl.BlockSpec(st_block, lambda b: (layer, b, 0, 0, 0)),
    ]
    args = [proj, proj, og, gk, onorm_g.reshape(1, GLA_DV), state_all]
    aliases = {}
    if states_out is not None:
        in_specs.append(pl.BlockSpec(memory_space=pl.ANY))
        args.append(states_out)
        aliases = {len(args) - 1: 1}
    return pl.pallas_call(
        functools.partial(_gla_sample_kernel, aliased=states_out is not None),
        grid=(DEC_BATCH,),
        in_specs=in_specs,
        out_specs=[
            pl.BlockSpec((DEC_SEQ, GLA_HV), lambda b: (b, 0)),
            pl.BlockSpec(st_block, lambda b: (layer, b, 0, 0, 0)),
        ],
        out_shape=[
            jax.ShapeDtypeStruct((NS_TOK, GLA_HV), BF16),
            jax.ShapeDtypeStruct((n_a, DEC_BATCH, GLA_H, GLA_DK, GLA_DV), F32),
        ],
        input_output_aliases=aliases,
        compiler_params=_params(("arbitrary",)),
        name="gla_sample",
    )(*args)


def _rope_partner(x):
    lane = lax.broadcasted_iota(jnp.int32, x.shape, 1)
    return jnp.where((lane & (MLA_ROPE - 1)) < MLA_ROPE // 2, pltpu.roll(x, x.shape[1] - MLA_ROPE // 2, 1),
                     pltpu.roll(x, MLA_ROPE // 2, 1))


def _kvprep_kernel(ckv_ref, kr_ref, g_ref, ckvn_ref, krr_ref, c4_ref, s4_ref, *, np_tiles):
    i = pl.program_id(0)
    x = ckv_ref[...]
    ms = jnp.mean(x * x, axis=-1, keepdims=True)
    ckvn_ref[...] = x * lax.rsqrt(ms + EPS) * g_ref[...]

    r = i * TM + lax.broadcasted_iota(jnp.int32, (TM, 128), 0)
    pos = jnp.where(i < np_tiles, r & (SEQ - 1), PAST + (r & (DEC_SEQ - 1))).astype(F32)
    k2 = (lax.broadcasted_iota(jnp.int32, (8, 128), 1) & (MLA_ROPE // 2 - 1)) * 2
    inv = jnp.power(jnp.float32(ROPE_BASE), -k2.astype(F32) / MLA_ROPE)[0:1, :]
    ang = pos * inv
    lane = lax.broadcasted_iota(jnp.int32, (TM, 128), 1)
    c4 = jnp.cos(ang)
    s4 = jnp.where((lane & (MLA_ROPE - 1)) < MLA_ROPE // 2, -jnp.sin(ang), jnp.sin(ang))
    c4_ref[...] = c4
    s4_ref[...] = s4
    kr = kr_ref[...]
    krr_ref[...] = (kr * c4 + _rope_partner(kr) * s4)[:, :MLA_ROPE]


def mla_kvprep(proj, kv_norm_g):
    np_tiles = NP_TOK // TM
    return pl.pallas_call(
        functools.partial(_kvprep_kernel, np_tiles=np_tiles),
        grid=(N_TOK // TM,),
        in_specs=[
            pl.BlockSpec((TM, MLA_RANK), lambda i: (i, 3072 // MLA_RANK)),
            pl.BlockSpec((TM, 128), lambda i: (i, 5632 // 128)),
            pl.BlockSpec((1, MLA_RANK), lambda i: (0, 0)),
        ],
        out_specs=[
            pl.BlockSpec((TM, MLA_RANK), lambda i: (i, 0)),
            pl.BlockSpec((TM, MLA_ROPE), lambda i: (i, 0)),
            pl.BlockSpec((TM, 128), lambda i: (i, 0)),
            pl.BlockSpec((TM, 128), lambda i: (i, 0)),
        ],
        out_shape=[
            jax.ShapeDtypeStruct((N_TOK, MLA_RANK), F32),
            jax.ShapeDtypeStruct((N_TOK, MLA_ROPE), F32),
            jax.ShapeDtypeStruct((N_TOK, 128), F32),
            jax.ShapeDtypeStruct((N_TOK, 128), F32),
        ],
        compiler_params=_params(("arbitrary",)),
        name="mla_kvprep",
    )(proj, proj, kv_norm_g.reshape(1, MLA_RANK))


def _kvup_kernel(ckv_ref, kr_ref, wuk_ref, wuv_ref, k_ref, v_ref):
    c = ckv_ref[...].astype(BF16)
    kn = _dot(c, wuk_ref[...]).astype(BF16)
    vv = _dot(c, wuv_ref[...]).astype(BF16)
    kr = kr_ref[...].astype(BF16)
    for h in range(MLA_H):
        k_ref[h, :, :MLA_NOPE] = kn[:, h * MLA_NOPE:(h + 1) * MLA_NOPE]
        k_ref[h, :, MLA_NOPE:] = kr
        v_ref[h] = vv[:, h * MLA_VH:(h + 1) * MLA_VH]


def mla_kvup(ckv_n, kr_r, w_uk2, w_uv2):
    return pl.pallas_call(
        _kvup_kernel,
        grid=(NP_TOK // TM,),
        in_specs=[
            pl.BlockSpec((TM, MLA_RANK), lambda i: (i, 0)),
            pl.BlockSpec((TM, MLA_ROPE), lambda i: (i, 0)),
            pl.BlockSpec((MLA_RANK, MLA_H * MLA_NOPE), lambda i: (0, 0)),
            pl.BlockSpec((MLA_RANK, MLA_H * MLA_VH), lambda i: (0, 0)),
        ],
        out_specs=[
            pl.BlockSpec((MLA_H, TM, MLA_QK), lambda i: (0, i, 0)),
            pl.BlockSpec((MLA_H, TM, MLA_VH), lambda i: (0, i, 0)),
        ],
        out_shape=[
            jax.ShapeDtypeStruct((MLA_H, NP_TOK, MLA_QK), BF16),
            jax.ShapeDtypeStruct((MLA_H, NP_TOK, MLA_VH), BF16),
        ],
        compiler_params=_params(("arbitrary",)),
        name="mla_kvup",
    )(ckv_n, kr_r, w_uk2, w_uv2)


FLASH_T = 512


def _rope_pair_select(x2, h):
    return jnp.where(h % 2 == 1, pltpu.roll(x2, MLA_ROPE, 1), x2)[:, :MLA_ROPE]


FLASH_G = 4


def _lanes(x, n):
    return x if n == 128 else jnp.concatenate([x] * (n // 128), axis=1)


def _flash_kernel(qt_ref, kt_ref, qn_ref, qr_ref, c4_ref, s4_ref, k_ref, v_ref, gate_ref, o_ref,
                  q_sc, m_sc, l_sc, acc_sc):
    s_id = pl.program_id(2)
    qi = qt_ref[s_id]
    ki = kt_ref[s_id]
    T, G = FLASH_T, FLASH_G

    @pl.when(ki == 0)
    def _():
        x = qr_ref[...]
        c = _lanes(c4_ref[...], G * MLA_ROPE)
        s = _lanes(s4_ref[...], G * MLA_ROPE)
        rot = (x * c + _rope_partner(x) * s) * MLA_SCALE
        for g in range(G):
            q_sc[g, :, :MLA_NOPE] = (qn_ref[:, g * MLA_NOPE:(g + 1) * MLA_NOPE] * MLA_SCALE).astype(BF16)
            q_sc[g, :, MLA_NOPE:] = rot[:, g * MLA_ROPE:(g + 1) * MLA_ROPE].astype(BF16)
        m_sc[...] = jnp.full_like(m_sc, _NEG)
        l_sc[...] = jnp.zeros_like(l_sc)
        acc_sc[...] = jnp.zeros_like(acc_sc)

    def step(masked):
        for g in range(G):
            s = _dot_nt(q_sc[g], k_ref[g])
            if masked:
                r = lax.broadcasted_iota(jnp.int32, (T, T), 0)
                c = lax.broadcasted_iota(jnp.int32, (T, T), 1)
                s = jnp.where(c <= r, s, _NEG)
            m_prev = m_sc[g]
            m_new = jnp.maximum(m_prev, jnp.max(s, axis=-1, keepdims=True))
            alpha = jnp.exp(m_prev - m_new)
            p = jnp.exp(s - _lanes(m_new, T))
            l_sc[g] = alpha * l_sc[g] + jnp.sum(p, axis=-1, keepdims=True)
            acc_sc[g] = alpha * acc_sc[g] + _dot(p.astype(BF16), v_ref[g])
            m_sc[g] = m_new

    @pl.when(ki < qi)
    def _():
        step(False)

    @pl.when(ki == qi)
    def _():
        step(True)
        for g in range(G):
            sl = slice(g * MLA_VH, (g + 1) * MLA_VH)
            o = acc_sc[g] / l_sc[g]
            o_ref[:, sl] = (o * _silu(gate_ref[:, sl])).astype(BF16)


def mla_flash(proj, c4, s4, kcat, vv):
    T, G = FLASH_T, FLASH_G
    nq = SEQ // T
    pairs = [(qi, ki) for qi in range(nq) for ki in range(qi + 1)]
    qt = jnp.asarray(np.array([p[0] for p in pairs], np.int32))
    kt = jnp.asarray(np.array([p[1] for p in pairs], np.int32))

    def qrow(b, g, s, qt, kt):
        return b * nq + qt[s]

    def krow(b, g, s, qt, kt):
        return b * nq + kt[s]

    grid_spec = pltpu.PrefetchScalarGridSpec(
        num_scalar_prefetch=2,
        grid=(BATCH, MLA_H // G, len(pairs)),
        in_specs=[
            pl.BlockSpec((T, G * MLA_NOPE), lambda b, g, s, qt, kt: (qrow(b, g, s, qt, kt), g)),
            pl.BlockSpec((T, G * MLA_ROPE),
                         lambda b, g, s, qt, kt: (qrow(b, g, s, qt, kt), 2048 // (G * MLA_ROPE) + g)),
            pl.BlockSpec((T, 128), lambda b, g, s, qt, kt: (qrow(b, g, s, qt, kt), 0)),
            pl.BlockSpec((T, 128), lambda b, g, s, qt, kt: (qrow(b, g, s, qt, kt), 0)),
            pl.BlockSpec((G, T, MLA_QK), lambda b, g, s, qt, kt: (g, krow(b, g, s, qt, kt), 0)),
            pl.BlockSpec((G, T, MLA_VH), lambda b, g, s, qt, kt: (g, krow(b, g, s, qt, kt), 0)),
            pl.BlockSpec((T, G * MLA_VH),
                         lambda b, g, s, qt, kt: (qrow(b, g, s, qt, kt), 3584 // (G * MLA_VH) + g)),
        ],
        out_specs=pl.BlockSpec((T, G * MLA_VH), lambda b, g, s, qt, kt: (qrow(b, g, s, qt, kt), g)),
        scratch_shapes=[
            pltpu.VMEM((G, T, MLA_QK), BF16),
            pltpu.VMEM((G, T, 128), F32),
            pltpu.VMEM((G, T, 128), F32),
            pltpu.VMEM((G, T, MLA_VH), F32),
        ],
    )
    return pl.pallas_call(
        _flash_kernel,
        grid_spec=grid_spec,
        out_shape=jax.ShapeDtypeStruct((NP_TOK, MLA_H * MLA_VH), BF16),
        compiler_params=_params(("arbitrary", "arbitrary", "arbitrary")),
        name="mla_flash",
    )(qt, kt, proj, proj, c4, s4, kcat, vv, proj)


def _qabs_kernel(qn_ref, qr_ref, c4_ref, s4_ref, wukt_ref, o_ref):
    h = pl.program_id(0)
    ql = _dot(qn_ref[...].astype(BF16), wukt_ref[0]) * MLA_SCALE
    x2 = qr_ref[...]
    rot = x2 * c4_ref[...] + _rope_partner(x2) * s4_ref[...]
    qr = _rope_pair_select(rot, h) * MLA_SCALE
    o_ref[:, 0, :, :MLA_RANK] = ql.reshape(DEC_BATCH, DEC_SEQ, MLA_RANK)
    o_ref[:, 0, :, MLA_RANK:] = qr.reshape(DEC_BATCH, DEC_SEQ, MLA_ROPE)


def mla_qabs(proj, c4, s4, w_ukt3):
    rb = NP_TOK // NS_TOK
    return pl.pallas_call(
        _qabs_kernel,
        grid=(MLA_H,),
        in_specs=[
            pl.BlockSpec((NS_TOK, 128), lambda h: (rb, h)),
            pl.BlockSpec((NS_TOK, 128), lambda h: (rb, 2048 // 128 + h // 2)),
            pl.BlockSpec((NS_TOK, 128), lambda h: (rb, 0)),
            pl.BlockSpec((NS_TOK, 128), lambda h: (rb, 0)),
            pl.BlockSpec((1, MLA_NOPE, MLA_RANK), lambda h: (h, 0, 0)),
        ],
        out_specs=pl.BlockSpec((DEC_BATCH, 1, DEC_SEQ, MLA_LAT), lambda h: (0, h, 0, 0)),
        out_shape=jax.ShapeDtypeStruct((DEC_BATCH, MLA_H, DEC_SEQ, MLA_LAT), F32),
        compiler_params=_params(("arbitrary",)),
        name="mla_qabs",
    )(proj, proj, c4, s4, w_ukt3)


DEC_SEQS = 2
DEC_PG = 8
DEC_NG = N_PAGES // DEC_PG
DEC_SLOTS = 4
DEC_AHEAD = DEC_SLOTS - 1


def _decode_kernel(pt_ref, q_ref, cnew_ref, knew_ref, ckv_hbm, krt_hbm, o_ref,
                   kv_buf, kr_buf, sem, m_sc, l_sc, acc_sc, *, layer):
    step = pl.program_id(0)
    n_groups = pl.num_programs(0) * DEC_NG
    R = MLA_H * DEC_SEQ

    def group_copies(n, slot):
        first_page = lax.div(n, DEC_NG) * (DEC_SEQS * N_PAGES) + lax.rem(n, DEC_NG) * DEC_PG
        cps = []
        for s in range(DEC_SEQS):
            for r in range(DEC_PG):
                page = pt_ref[first_page + s * N_PAGES + r]
                j = s * DEC_PG + r
                cps.append(pltpu.make_async_copy(ckv_hbm.at[layer, page], kv_buf.at[slot, j], sem.at[0, slot]))
                cps.append(pltpu.make_async_copy(krt_hbm.at[layer, page], kr_buf.at[slot, j], sem.at[1, slot]))
        return cps

    def start_group(n, slot):
        for i, cp in enumerate(group_copies(n, slot)):
            cp.start(priority=(i // 2) % 2)

    @pl.when(step == 0)
    def _():
        for n0 in range(DEC_AHEAD):
            start_group(n0, n0)

    m_sc[...] = jnp.full_like(m_sc, _NEG)
    l_sc[...] = jnp.zeros_like(l_sc)
    acc_sc[...] = jnp.zeros_like(acc_sc)

    ql, qr = [], []
    for s in range(DEC_SEQS):
        q = q_ref[s].reshape(R, MLA_LAT)
        ql.append(q[:, :MLA_RANK].astype(BF16))
        qr.append(q[:, MLA_RANK:].astype(BF16))

    def online(s, scores, vals):
        m_prev = m_sc[s]
        m_new = jnp.maximum(m_prev, jnp.max(scores, axis=-1, keepdims=True))
        alpha = jnp.exp(m_prev - m_new)
        p = jnp.exp(scores - _lanes(m_new, scores.shape[1]))
        l_sc[s] = alpha * l_sc[s] + jnp.sum(p, axis=-1, keepdims=True)
        acc_sc[s] = _lanes(alpha, MLA_RANK) * acc_sc[s] + _dot(p.astype(BF16), vals)
        m_sc[s] = m_new

    def group(g, carry):
        n = step * DEC_NG + g
        slot = lax.rem(n, DEC_SLOTS)
        for cp in group_copies(n, slot):
            cp.wait()

        @pl.when(n + DEC_AHEAD < n_groups)
        def _():
            start_group(n + DEC_AHEAD, lax.rem(n + DEC_AHEAD, DEC_SLOTS))

        for s in range(DEC_SEQS):
            kv = jnp.concatenate(
                [kv_buf[slot, s * DEC_PG + r].astype(BF16) for r in range(DEC_PG)], axis=0)
            krt = jnp.concatenate(
                [kr_buf[slot, s * DEC_PG + r].astype(BF16) for r in range(DEC_PG)], axis=1)
            online(s, _dot_nt(ql[s], kv) + _dot(qr[s], krt), kv)
        return carry

    lax.fori_loop(0, DEC_NG, group, 0)

    t = lax.broadcasted_iota(jnp.int32, (R, PAGE), 0) & (DEC_SEQ - 1)
    j = lax.broadcasted_iota(jnp.int32, (R, PAGE), 1)
    for s in range(DEC_SEQS):
        rows = slice(s * DEC_SEQ, (s + 1) * DEC_SEQ)
        cn = jnp.concatenate([cnew_ref[rows, :], jnp.zeros((PAGE - DEC_SEQ, MLA_RANK), F32)], axis=0).astype(BF16)
        kn = jnp.concatenate([knew_ref[rows, :], jnp.zeros((PAGE - DEC_SEQ, MLA_ROPE), F32)], axis=0).astype(BF16)
        sc = _dot_nt(ql[s], cn) + _dot_nt(qr[s], kn)
        online(s, jnp.where(j <= t, sc, _NEG), cn)
        o = acc_sc[s] / _lanes(l_sc[s], MLA_RANK)
        o_ref[s] = o.reshape(MLA_H, DEC_SEQ, MLA_RANK)


def mla_decode(qcat, cache_ckv, cache_krt, layer, page_table, ckv_n, kr_r):
    rows = DEC_SEQS * DEC_SEQ
    r0 = NP_TOK // rows
    n_pg = DEC_SEQS * DEC_PG
    R = MLA_H * DEC_SEQ
    grid_spec = pltpu.PrefetchScalarGridSpec(
        num_scalar_prefetch=1,
        grid=(DEC_BATCH // DEC_SEQS,),
        in_specs=[
            pl.BlockSpec((DEC_SEQS, MLA_H, DEC_SEQ, MLA_LAT), lambda i, pt: (i, 0, 0, 0)),
            pl.BlockSpec((rows, MLA_RANK), lambda i, pt: (r0 + i, 0)),
            pl.BlockSpec((rows, MLA_ROPE), lambda i, pt: (r0 + i, 0)),
            pl.BlockSpec(memory_space=pl.ANY),
            pl.BlockSpec(memory_space=pl.ANY),
        ],
        out_specs=pl.BlockSpec((DEC_SEQS, MLA_H, DEC_SEQ, MLA_RANK), lambda i, pt: (i, 0, 0, 0)),
        scratch_shapes=[
            pltpu.VMEM((DEC_SLOTS, n_pg, PAGE, MLA_RANK), F32),
            pltpu.VMEM((DEC_SLOTS, n_pg, MLA_ROPE, PAGE), F32),
            pltpu.SemaphoreType.DMA((2, DEC_SLOTS)),
            pltpu.VMEM((DEC_SEQS, R, 128), F32),
            pltpu.VMEM((DEC_SEQS, R, 128), F32),
            pltpu.VMEM((DEC_SEQS, R, MLA_RANK), F32),
        ],
    )
    return pl.pallas_call(
        functools.partial(_decode_kernel, layer=layer),
        grid_spec=grid_spec,
        out_shape=jax.ShapeDtypeStruct((DEC_BATCH, MLA_H, DEC_SEQ, MLA_RANK), F32),
        compiler_params=_params(("arbitrary",)),
        name="mla_decode",
    )(page_table.reshape(-1), qcat, ckv_n, kr_r, cache_ckv, cache_krt)


def _uvup_kernel(ol_ref, wuv_ref, gate_ref, a_ref):
    ol = ol_ref[...].reshape(NS_TOK, MLA_RANK).astype(BF16)
    a_ref[...] = (_dot(ol, wuv_ref[0]) * _silu(gate_ref[...])).astype(BF16)


def mla_uvup(o_lat, w_uv3, proj):
    rb = NP_TOK // NS_TOK
    return pl.pallas_call(
        _uvup_kernel,
        grid=(MLA_H,),
        in_specs=[
            pl.BlockSpec((DEC_BATCH, 1, DEC_SEQ, MLA_RANK), lambda h: (0, h, 0, 0)),
            pl.BlockSpec((1, MLA_RANK, MLA_VH), lambda h: (h, 0, 0)),
            pl.BlockSpec((NS_TOK, 128), lambda h: (rb, 3584 // 128 + h)),
        ],
        out_specs=pl.BlockSpec((NS_TOK, MLA_VH), lambda h: (0, h)),
        out_shape=jax.ShapeDtypeStruct((NS_TOK, MLA_H * MLA_VH), BF16),
        compiler_params=_params(("arbitrary",)),
        name="mla_uvup",
    )(o_lat, w_uv3, proj)


POOL_HALO = 16


def _pool_prompt_kernel(u_ref, halo_ref, gate_ref, wg_ref, sc_ref, a_ref, tail_ref, ext_sc, *, tiles_per_seq):
    i = pl.program_id(0)
    g = pl.program_id(1)
    first = (i % tiles_per_seq) == 0
    u = u_ref[...]
    tail_ref[0] = u[TM - POOL_HALO:, :]
    ext_sc[:POOL_HALO, :] = jnp.where(first, 0.0, halo_ref[...])
    ext_sc[POOL_HALO:, :] = u
    t = (i % tiles_per_seq) * TM + lax.broadcasted_iota(jnp.int32, (TM, 1), 0)

    for gi, w in enumerate(POOL_WINDOWS):
        @pl.when(g == gi)
        def _(w=w):
            acc = u
            for j in range(1, w):
                acc = acc + ext_sc[POOL_HALO - j:POOL_HALO - j + TM, :]
            cnt = jnp.minimum(t + 1, w).astype(F32)
            p = (acc / cnt - u).astype(BF16)
            z = _dot(p, wg_ref[0]) * sc_ref[...]
            a_ref[...] = (z * _silu(gate_ref[...])).astype(BF16)


def pool_prompt(proj, w_grp, pscale):
    tiles_per_seq = SEQ // TM
    hb = TM // POOL_HALO
    return pl.pallas_call(
        functools.partial(_pool_prompt_kernel, tiles_per_seq=tiles_per_seq),
        grid=(NP_TOK // TM, len(POOL_WINDOWS)),
        in_specs=[
            pl.BlockSpec((TM, POOL_G), lambda i, g: (i, g)),
            pl.BlockSpec((POOL_HALO, POOL_G), lambda i, g: (jnp.maximum(i * hb - 1, 0), g)),
            pl.BlockSpec((TM, POOL_G), lambda i, g: (i, len(POOL_WINDOWS) + g)),
            pl.BlockSpec((1, POOL_G, POOL_G), lambda i, g: (g, 0, 0)),
            pl.BlockSpec((1, POOL_G), lambda i, g: (0, g)),
        ],
        out_specs=[
            pl.BlockSpec((TM, POOL_G), lambda i, g: (i, g)),
            pl.BlockSpec((1, POOL_HALO, POOL_G), lambda i, g: (i, 0, g)),
        ],
        out_shape=[
            jax.ShapeDtypeStruct((NP_TOK, D), BF16),
            jax.ShapeDtypeStruct((NP_TOK // TM, POOL_HALO, D), F32),
        ],
        scratch_shapes=[pltpu.VMEM((POOL_HALO + TM, POOL_G), F32)],
        compiler_params=_params(("arbitrary", "arbitrary")),
        name="pool_prompt",
    )(proj, proj, proj, w_grp, pscale)


def _pool_sample_kernel(u_ref, gate_ref, hist_ref, wg_ref, sc_ref, a_ref):
    g = pl.program_id(0)

    def seq(r):
        return hist_ref[r] if r < POOL_HIST else u_ref[r - POOL_HIST]

    for gi, w in enumerate(POOL_WINDOWS):
        @pl.when(g == gi)
        def _(w=w):
            ps = []
            for t in range(DEC_SEQ):
                acc = seq(POOL_HIST + t)
                for j in range(1, w):
                    acc = acc + seq(POOL_HIST + t - j)
                ps.append(acc / float(w) - u_ref[t])
            p = jnp.concatenate(ps, axis=0).astype(BF16)
            z = _dot(p, wg_ref[0]) * sc_ref[...]
            gate = gate_ref[...].reshape(NS_TOK, POOL_G)
            a_ref[...] = (z * _silu(gate)).astype(BF16).reshape(DEC_SEQ, DEC_BATCH, POOL_G)


def pool_sample(proj_t, hist_t, w_grp, pscale):
    ng = len(POOL_WINDOWS)
    return pl.pallas_call(
        _pool_sample_kernel,
        grid=(ng,),
        in_specs=[
            pl.BlockSpec((DEC_SEQ, DEC_BATCH, POOL_G), lambda g: (0, 0, g)),
            pl.BlockSpec((DEC_SEQ, DEC_BATCH, POOL_G), lambda g: (0, 0, ng + g)),
            pl.BlockSpec((POOL_HIST, DEC_BATCH, POOL_G), lambda g: (0, 0, g)),
            pl.BlockSpec((1, POOL_G, POOL_G), lambda g: (g, 0, 0)),
            pl.BlockSpec((1, POOL_G), lambda g: (0, g)),
        ],
        out_specs=pl.BlockSpec((DEC_SEQ, DEC_BATCH, POOL_G), lambda g: (0, 0, g)),
        out_shape=jax.ShapeDtypeStruct((DEC_SEQ, DEC_BATCH, D), BF16),
        compiler_params=_params(("arbitrary",)),
        name="pool_sample",
    )(proj_t, proj_t, hist_t, w_grp, pscale)


def kernel(x_prompt, x_sample, c_prompt, c_sample, state_gla, cache_ckv, cache_kr, state_pool, page_table, norm_g, ada_w, ada_b, final_norm_g, gla_w_in, gla_w_gate_up, gla_b_gate, gla_onorm_g, gla_w_out, mla_w_in, mla_kv_norm_g, mla_w_uk, mla_w_uv, mla_w_out, pool_w_in, pool_w_grp, pool_scale, pool_w_out):
    x_all = (x_prompt.reshape(NP_TOK, D), x_sample.reshape(NS_TOK, D))

    n_c = BATCH + DEC_BATCH
    c_all = jnp.concatenate([c_prompt, c_sample, jnp.zeros((8 - n_c % 8, D), F32)], axis=0)
    mod = ada_mod(c_all, ada_w, ada_b)

    gla_states_p, gla_states_s = [], None
    ckv_rows, kr_rows, pool_p, pool_s = [], [], [], []
    ia = ib = ic = 0
    mods = [(mod[l, :BATCH].reshape(BATCH, 1, 3 * D), jnp.repeat(mod[l, BATCH:n_c], DEC_SEQ, axis=0))
            for l in range(DEPTH)]
    h = norm_mod(x_all, norm_g[0], *mods[0])
    for l in range(DEPTH):
        mod_p, mod_s = mods[l]
        mixer = LAYER_MIXER[l]
        if mixer == 0:
            w_t = jnp.swapaxes(gla_w_in[ia], 0, 1).astype(BF16)
            cut = 2 * GLA_HK + GLA_HV
            w_glr = jnp.pad(w_t[cut:cut + GLA_RANK], ((0, 128 - GLA_RANK), (0, 0)))
            w_up = jnp.pad(gla_w_gate_up[ia], ((0, 128 - GLA_RANK), (0, 0))).astype(BF16)
            proj = proj_matmul(h, w_t, 1024, True, n=cut)
            og = proj_matmul(h, w_t[cut + GLA_RANK:], 1024, True)
            gk = gla_gate(h, w_glr, w_up, gla_b_gate[ia].reshape(1, GLA_HK))
            a_p, st_p = gla_prompt(proj, og, gk, gla_onorm_g[ia])
            a_s, gla_states_s = gla_sample(proj, og, gk, gla_onorm_g[ia], state_gla, ia, states_out=gla_states_s)
            gla_states_p.append(st_p)
            w_out = gla_w_out[ia].astype(BF16)
            ia += 1
        elif mixer == 1:
            w_t = jnp.swapaxes(mla_w_in[ib], 0, 1)
            nq = MLA_H * MLA_QK
            wq = w_t[:nq].reshape(MLA_H, MLA_QK, D)
            w_perm = jnp.concatenate([
                wq[:, :MLA_NOPE].reshape(MLA_H * MLA_NOPE, D),
                wq[:, MLA_NOPE:].reshape(MLA_H * MLA_ROPE, D),
                w_t[nq:nq + MLA_RANK],
                w_t[nq + MLA_RANK + MLA_ROPE:],
                w_t[nq + MLA_RANK:nq + MLA_RANK + MLA_ROPE],
                jnp.zeros((128 - MLA_ROPE, D), F32),
            ], axis=0).astype(BF16)
            proj = proj_matmul(h, w_perm, 1152, True)
            ckv_n, kr_r, c4, s4 = mla_kvprep(proj, mla_kv_norm_g[ib])
            w_uk2 = mla_w_uk[ib].reshape(MLA_RANK, MLA_H * MLA_NOPE).astype(BF16)
            w_uv2 = mla_w_uv[ib].reshape(MLA_RANK, MLA_H * MLA_VH).astype(BF16)
            w_ukt3 = jnp.transpose(mla_w_uk[ib], (1, 2, 0)).astype(BF16)
            w_uv3 = jnp.transpose(mla_w_uv[ib], (1, 0, 2)).astype(BF16)
            kcat, vv = mla_kvup(ckv_n, kr_r, w_uk2, w_uv2)
            a_p = mla_flash(proj, c4, s4, kcat, vv)
            qcat = mla_qabs(proj, c4, s4, w_ukt3)
            cache_krt = jnp.swapaxes(cache_kr, 2, 3)
            o_lat = mla_decode(qcat, cache_ckv, cache_krt, ib, page_table, ckv_n, kr_r)
            a_s = mla_uvup(o_lat, w_uv3, proj)
            ckv_rows.append(ckv_n)
            kr_rows.append(kr_r)
            w_out = mla_w_out[ib].astype(BF16)
            ib += 1
        else:
            proj = proj_matmul(h, pool_w_in[ic].astype(BF16), 1024, False)
            w_grp = pool_w_grp[ic].astype(BF16)
            pscale = pool_scale[ic].reshape(1, D)
            a_p, u_tails = pool_prompt(proj, w_grp, pscale)
            proj_t = jnp.transpose(proj[NP_TOK:].reshape(DEC_BATCH, DEC_SEQ, 2 * D), (1, 0, 2))
            hist_t = jnp.transpose(state_pool[ic], (1, 0, 2))
            a_st = pool_sample(proj_t, hist_t, w_grp, pscale)
            a_s = jnp.transpose(a_st, (1, 0, 2)).reshape(NS_TOK, D)
            u_s = proj[NP_TOK:, :D].reshape(DEC_BATCH, DEC_SEQ, D)
            pool_p.append(u_tails.reshape(BATCH, SEQ // TM, POOL_HALO, D)[:, -1, POOL_HALO - POOL_HIST:, :])
            pool_s.append(jnp.concatenate([state_pool[ic][:, DEC_SEQ:, :], u_s], axis=1))
            w_out = pool_w_out[ic].astype(BF16)
            ic += 1
        if l == DEPTH - 1:
            y_p, y_s = out_proj(a_p, a_s, w_out, x_all, mod_p, mod_s, final_g=final_norm_g)
        else:
            x_all, h = out_proj(a_p, a_s, w_out, x_all, mod_p, mod_s, next_norm=(norm_g[l + 1],) + mods[l + 1])

    y_prompt = y_p.reshape(BATCH, SEQ, D)
    y_sample = y_s.reshape(DEC_BATCH, DEC_SEQ, D)
    ckv_all = jnp.stack(ckv_rows)
    kr_all = jnp.stack(kr_rows)
    return (
        y_prompt,
        y_sample,
        jnp.stack(gla_states_p),
        gla_states_s,
        ckv_all[:, :NP_TOK].reshape(-1, BATCH, SEQ, MLA_RANK),
        kr_all[:, :NP_TOK].reshape(-1, BATCH, SEQ, MLA_ROPE),
        ckv_all[:, NP_TOK:].reshape(-1, DEC_BATCH, DEC_SEQ, MLA_RANK),
        kr_all[:, NP_TOK:].reshape(-1, DEC_BATCH, DEC_SEQ, MLA_ROPE),
        jnp.stack(pool_p),
        jnp.stack(pool_s),
    )
```

```python
import functools

import numpy as np
import jax
import jax.numpy as jnp
from jax import lax
from jax.experimental import pallas as pl
from jax.experimental.pallas import tpu as pltpu

F32 = jnp.float32
BF16 = jnp.bfloat16

D = 2048
BATCH = 4
SEQ = 2048
DEC_BATCH = 128
DEC_SEQ = 8
PAGE = 128
N_PAGES = 64
PAST = N_PAGES * PAGE
DEPTH = 4
LAYER_MIXER = (0, 1, 2, 0)
EPS = 1e-6

NP_TOK = BATCH * SEQ
NS_TOK = DEC_BATCH * DEC_SEQ
N_TOK = NP_TOK + NS_TOK

GLA_H = 4
GLA_DK = 256
GLA_DV = 512
GLA_RANK = 16
GLA_TAU = 16.0
GLA_HK = GLA_H * GLA_DK
GLA_HV = GLA_H * GLA_DV
GLA_C = 128
GLA_SUB = 16

MLA_H = 16
MLA_NOPE = 128
MLA_ROPE = 64
MLA_VH = 128
MLA_RANK = 512
MLA_SCALE = (MLA_NOPE + MLA_ROPE) ** -0.5
MLA_QK = MLA_NOPE + MLA_ROPE
MLA_LAT = MLA_RANK + MLA_ROPE
MLA_N = 5760
ROPE_BASE = 10000.0

POOL_WINDOWS = (2, 4, 8, 16)
POOL_G = 512
POOL_HIST = 15

TM = 512
TM_OUT = 256
VMEM_LIMIT = 56 * 1024 * 1024

_NEG = -1e30


def _params(sem):
    return pltpu.CompilerParams(dimension_semantics=sem, vmem_limit_bytes=VMEM_LIMIT)


def _silu(x):
    return x * (1.0 / (1.0 + jnp.exp(-x)))


def _dot(a, b):
    return jnp.dot(a, b, preferred_element_type=F32)


def _dot_nt(a, b):
    return lax.dot_general(a, b, (((1,), (1,)), ((), ())), preferred_element_type=F32)


def _ada_kernel(c_ref, w_ref, b_ref, o_ref):
    sc = _silu(c_ref[...]).astype(BF16)
    o_ref[0] = _dot(sc, w_ref[0].astype(BF16)) + b_ref[0]


def ada_mod(c_all, ada_w, ada_b):
    rows = c_all.shape[0]
    tn = 768
    return pl.pallas_call(
        _ada_kernel,
        grid=(DEPTH, 3 * D // tn),
        in_specs=[
            pl.BlockSpec((rows, D), lambda l, j: (0, 0)),
            pl.BlockSpec((1, D, tn), lambda l, j: (l, 0, j)),
            pl.BlockSpec((1, 1, tn), lambda l, j: (l, 0, j)),
        ],
        out_specs=pl.BlockSpec((1, rows, tn), lambda l, j: (l, 0, j)),
        out_shape=jax.ShapeDtypeStruct((DEPTH, rows, 3 * D), F32),
        compiler_params=_params(("arbitrary", "arbitrary")),
        name="ada_mod",
    )(c_all, ada_w, ada_b.reshape(DEPTH, 1, 3 * D))


def _mod_rows(i, np_tiles, p_ref, s_ref):
    return jnp.where(i < np_tiles, p_ref[0], s_ref[...])


def _norm_mod(x, g, shift, scale):
    ms = jnp.mean(x * x, axis=-1, keepdims=True)
    return ((x * lax.rsqrt(ms + EPS) * g) * (1.0 + scale) + shift).astype(BF16)


def _mod_specs(tm, cols):
    np_tiles = NP_TOK // tm
    tiles_per_seq = SEQ // tm
    specs = []
    for col in cols:
        specs.append(pl.BlockSpec((1, 1, D), lambda i, col=col: (jnp.minimum(i // tiles_per_seq, BATCH - 1), 0, col)))
        specs.append(pl.BlockSpec((tm, D), lambda i, col=col: (jnp.maximum(i - np_tiles, 0), col)))
    return specs


def _stream_args(tm, x):
    if not isinstance(x, tuple):
        return [pl.BlockSpec((tm, D), lambda i: (i, 0))], [x]
    np_tiles = NP_TOK // tm
    specs = [
        pl.BlockSpec((tm, D), lambda i: (jnp.minimum(i, np_tiles - 1), 0)),
        pl.BlockSpec((tm, D), lambda i: (jnp.maximum(i - np_tiles, 0), 0)),
    ]
    return specs, list(x)


def _stream_tile(i, np_tiles, x_refs):
    if len(x_refs) == 1:
        return x_refs[0][...]
    return jnp.where(i < np_tiles, x_refs[0][...], x_refs[1][...])


def _norm_kernel(*refs, np_tiles, n_x):
    x_refs, (g_ref, shp_ref, shs_ref, scp_ref, scs_ref, h_ref) = refs[:n_x], refs[n_x:]
    i = pl.program_id(0)
    x = _stream_tile(i, np_tiles, x_refs)
    h_ref[...] = _norm_mod(x, g_ref[...], _mod_rows(i, np_tiles, shp_ref, shs_ref),
                           _mod_rows(i, np_tiles, scp_ref, scs_ref))


def norm_mod(x, norm_g, mod_p, mod_s):
    tm = TM
    x_specs, x_args = _stream_args(tm, x)
    return pl.pallas_call(
        functools.partial(_norm_kernel, np_tiles=NP_TOK // tm, n_x=len(x_args)),
        grid=(N_TOK // tm,),
        in_specs=x_specs + [pl.BlockSpec((1, D), lambda i: (0, 0))] + _mod_specs(tm, (0, 1)),
        out_specs=pl.BlockSpec((tm, D), lambda i: (i, 0)),
        out_shape=jax.ShapeDtypeStruct((N_TOK, D), BF16),
        compiler_params=_params(("arbitrary",)),
        name="norm_mod",
    )(*x_args, norm_g.reshape(1, D), mod_p, mod_s, mod_p, mod_s)


TMM = 1024


def _mm_kernel(h_ref, w_ref, o_ref, *, w_is_nk):
    o_ref[...] = _dot_nt(h_ref[...], w_ref[...]) if w_is_nk else _dot(h_ref[...], w_ref[...])


def proj_matmul(h, w, tn, w_is_nk, n=None):
    if n is None:
        n = w.shape[0] if w_is_nk else w.shape[1]
    w_spec = pl.BlockSpec((tn, D), lambda j, i: (j, 0)) if w_is_nk else pl.BlockSpec((D, tn), lambda j, i: (0, j))
    return pl.pallas_call(
        functools.partial(_mm_kernel, w_is_nk=w_is_nk),
        grid=(n // tn, N_TOK // TMM),
        in_specs=[pl.BlockSpec((TMM, D), lambda j, i: (i, 0)), w_spec],
        out_specs=pl.BlockSpec((TMM, tn), lambda j, i: (i, j)),
        out_shape=jax.ShapeDtypeStruct((N_TOK, n), F32),
        compiler_params=_params(("arbitrary", "arbitrary")),
        name="proj_matmul",
    )(h, w)


def _gate_kernel(h_ref, wg_ref, wu_ref, bg_ref, gk_ref):
    glr = _dot_nt(h_ref[...], wg_ref[...]).astype(BF16)
    z = _dot(glr, wu_ref[...]) + bg_ref[...]
    gk_ref[...] = (jnp.minimum(z, 0.0) - jnp.log1p(jnp.exp(-jnp.abs(z)))) / GLA_TAU


def gla_gate(h, wg, wu, bg):
    tm = TM
    return pl.pallas_call(
        _gate_kernel,
        grid=(N_TOK // tm,),
        in_specs=[
            pl.BlockSpec((tm, D), lambda i: (i, 0)),
            pl.BlockSpec(wg.shape, lambda i: (0, 0)),
            pl.BlockSpec(wu.shape, lambda i: (0, 0)),
            pl.BlockSpec(bg.shape, lambda i: (0, 0)),
        ],
        out_specs=pl.BlockSpec((tm, GLA_HK), lambda i: (i, 0)),
        out_shape=jax.ShapeDtypeStruct((N_TOK, GLA_HK), F32),
        compiler_params=_params(("arbitrary",)),
        name="gla_gate",
    )(h, wg, wu, bg)


def _out_kernel(ap_ref, as_ref, w_ref, *rest, np_tiles, final, n_x):
    x_refs, (gp_ref, gs_ref), rest = rest[:n_x], rest[n_x:n_x + 2], rest[n_x + 2:]
    i = pl.program_id(0)
    a = jnp.where(i < np_tiles, ap_ref[...], as_ref[...])
    y = _dot(a, w_ref[...])
    gate = _mod_rows(i, np_tiles, gp_ref, gs_ref)
    xn = _stream_tile(i, np_tiles, x_refs) + gate * y
    if not final:
        ng_ref, shp_ref, shs_ref, scp_ref, scs_ref, o_ref, h_ref = rest
        o_ref[...] = xn
        h_ref[...] = _norm_mod(xn, ng_ref[...], _mod_rows(i, np_tiles, shp_ref, shs_ref),
                               _mod_rows(i, np_tiles, scp_ref, scs_ref))
        return
    fg_ref, yp_ref, ys_ref = rest
    ms = jnp.mean(xn * xn, axis=-1, keepdims=True)
    yn = xn * lax.rsqrt(ms + EPS) * fg_ref[...]

    @pl.when(i < np_tiles)
    def _():
        yp_ref[...] = yn

    @pl.when(i >= np_tiles)
    def _():
        ys_ref[...] = yn


def out_proj(a_p, a_s, w_out, x, mod_p, mod_s, final_g=None, next_norm=None):
    tm = TM_OUT
    np_tiles = NP_TOK // tm
    final = final_g is not None
    x_specs, x_args = _stream_args(tm, x)
    in_specs = [
        pl.BlockSpec((tm, D), lambda i: (jnp.minimum(i, np_tiles - 1), 0)),
        pl.BlockSpec((tm, D), lambda i: (jnp.maximum(i - np_tiles, 0), 0)),
        pl.BlockSpec((D, D), lambda i: (0, 0)),
    ] + x_specs + _mod_specs(tm, (2,))
    args = [a_p, a_s, w_out] + x_args + [mod_p, mod_s]
    if not final:
        ng, nmod_p, nmod_s = next_norm
        in_specs += [pl.BlockSpec((1, D), lambda i: (0, 0))] + _mod_specs(tm, (0, 1))
        args += [ng.reshape(1, D), nmod_p, nmod_s, nmod_p, nmod_s]
        out_specs = [pl.BlockSpec((tm, D), lambda i: (i, 0)), pl.BlockSpec((tm, D), lambda i: (i, 0))]
        out_shape = [jax.ShapeDtypeStruct((N_TOK, D), F32), jax.ShapeDtypeStruct((N_TOK, D), BF16)]
    else:
        in_specs.append(pl.BlockSpec((1, D), lambda i: (0, 0)))
        args.append(final_g.reshape(1, D))
        out_specs = [
            pl.BlockSpec((tm, D), lambda i: (jnp.minimum(i, np_tiles - 1), 0)),
            pl.BlockSpec((tm, D), lambda i: (jnp.maximum(i - np_tiles, 0), 0)),
        ]
        out_shape = [jax.ShapeDtypeStruct((NP_TOK, D), F32), jax.ShapeDtypeStruct((NS_TOK, D), F32)]
    return pl.pallas_call(
        functools.partial(_out_kernel, np_tiles=np_tiles, final=final, n_x=len(x_args)),
        grid=(N_TOK // tm,),
        in_specs=in_specs,
        out_specs=out_specs,
        out_shape=out_shape,
        compiler_params=_params(("arbitrary",)),
        name="out_proj_final" if final else "out_proj",
    )(*args)


def _cumsum_rows(x):
    n = x.shape[0]
    row = lax.broadcasted_iota(jnp.int32, x.shape, 0)
    s = 1
    while s < n:
        x = x + jnp.where(row >= s, pltpu.roll(x, s, 0), 0.0)
        s *= 2
    return x


LOG2E = 1.4426950408889634


def _cumsum_mxu(x):
    n = x.shape[0]
    r = lax.broadcasted_iota(jnp.int32, (n, n), 0)
    c = lax.broadcasted_iota(jnp.int32, (n, n), 1)
    tri = jnp.where(c <= r, 1.0, 0.0).astype(BF16)
    hi = x.astype(BF16)
    rem = x - hi.astype(F32)
    mid = rem.astype(BF16)
    lo = (rem - mid.astype(F32)).astype(BF16)
    return _dot(tri, hi) + _dot(tri, mid) + _dot(tri, lo)


def _col_from_row(v):
    return jnp.broadcast_to(v, (128, v.shape[1])).T


def _head_rmsnorm_gate(o, g, og):
    ms = jnp.mean(o * o, axis=-1, keepdims=True)
    return (o * lax.rsqrt(ms + EPS) * g) * _silu(og)


def _gla_chunk(q, k, v, gk, S):
    C, SUB = GLA_C, GLA_SUB
    nsub = C // SUB

    cum = _cumsum_mxu(gk) * LOG2E
    excl = cum - gk * LOG2E
    last = cum[C - 1:C, :]

    o = _dot((q * jnp.exp2(cum)).astype(BF16), S.astype(BF16))

    row_blocks = [jnp.zeros((SUB, C), F32)]
    for i in range(1, nsub):
        sl = slice(i * SUB, (i + 1) * SUB)
        n_k = i * SUB
        b_i = excl[n_k:n_k + 1, :]
        qt = (q[sl] * jnp.exp2(cum[sl] - b_i)).astype(BF16)
        kh = (k[:n_k] * jnp.exp2(b_i - cum[:n_k])).astype(BF16)
        kh = jnp.concatenate([kh, jnp.zeros((C - n_k, GLA_DK), BF16)], axis=0)
        row_blocks.append(_dot_nt(qt, kh))
    att = jnp.concatenate(row_blocks, axis=0)

    parts = []
    for d in range(SUB):
        k_d = k if d == 0 else pltpu.roll(k, d, 0)
        cum_d = cum if d == 0 else pltpu.roll(cum, d, 0)
        parts.append((q * k_d * jnp.exp2(cum - cum_d)).astype(BF16))
    ones = jnp.ones((GLA_DK, 128), BF16)
    band = _dot(jnp.concatenate(parts, axis=0), ones)
    r = lax.broadcasted_iota(jnp.int32, (C, C), 0)
    cc = lax.broadcasted_iota(jnp.int32, (C, C), 1)
    off = jnp.where((r & -SUB) == (cc & -SUB), r - cc, -1)
    for d in range(SUB):
        att = jnp.where(off == d, band[d * C:(d + 1) * C], att)

    o = o + _dot(att.astype(BF16), v)

    kd = (k * jnp.exp2(last - cum)).astype(BF16)
    dec = _col_from_row(jnp.exp2(last))
    dec = jnp.concatenate([dec] * (GLA_DV // 128), axis=1)
    return o, dec * S + _dot(kd.T, v)


def _gla_prompt_kernel(qk_ref, v_ref, og_ref, gk_ref, g_ref, a_ref, s_ref):
    @pl.when(pl.program_id(1) == 0)
    def _():
        s_ref[...] = jnp.zeros_like(s_ref)

    for h in range(GLA_H):
        ksl = slice(h * GLA_DK, (h + 1) * GLA_DK)
        vsl = slice(h * GLA_DV, (h + 1) * GLA_DV)
        q = qk_ref[:, ksl] * (GLA_DK ** -0.5)
        k = qk_ref[:, GLA_HK + h * GLA_DK:GLA_HK + (h + 1) * GLA_DK]
        o, s_new = _gla_chunk(q, k, v_ref[:, vsl].astype(BF16), gk_ref[:, ksl], s_ref[0, h])
        s_ref[0, h] = s_new
        a_ref[:, vsl] = _head_rmsnorm_gate(o, g_ref[...], og_ref[:, vsl]).astype(BF16)


def gla_prompt(proj, og, gk, onorm_g):
    nc = SEQ // GLA_C

    def rows(b, c):
        return b * nc + c

    return pl.pallas_call(
        _gla_prompt_kernel,
        grid=(BATCH, nc),
        in_specs=[
            pl.BlockSpec((GLA_C, 2 * GLA_HK), lambda b, c: (rows(b, c), 0)),
            pl.BlockSpec((GLA_C, GLA_HV), lambda b, c: (rows(b, c), 1)),
            pl.BlockSpec((GLA_C, GLA_HV), lambda b, c: (rows(b, c), 0)),
            pl.BlockSpec((GLA_C, GLA_HK), lambda b, c: (rows(b, c), 0)),
            pl.BlockSpec((1, GLA_DV), lambda b, c: (0, 0)),
        ],
        out_specs=[
            pl.BlockSpec((GLA_C, GLA_HV), lambda b, c: (rows(b, c), 0)),
            pl.BlockSpec((1, GLA_H, GLA_DK, GLA_DV), lambda b, c: (b, 0, 0, 0)),
        ],
        out_shape=[
            jax.ShapeDtypeStruct((NP_TOK, GLA_HV), BF16),
            jax.ShapeDtypeStruct((BATCH, GLA_H, GLA_DK, GLA_DV), F32),
        ],
        compiler_params=_params(("arbitrary", "arbitrary")),
        name="gla_prompt",
    )(proj, proj, og, gk, onorm_g.reshape(1, GLA_DV))


def _gla_sample_kernel(qk_ref, v_ref, og_ref, gk_ref, g_ref, s0_ref, *rest, aliased):
    a_ref, s_ref = rest[1:] if aliased else rest
    T = DEC_SEQ
    row = lax.broadcasted_iota(jnp.int32, (T, GLA_DV), 0)
    outs = []
    for h in range(GLA_H):
        ksl = slice(h * GLA_DK, (h + 1) * GLA_DK)
        vsl = slice(h * GLA_DV, (h + 1) * GLA_DV)
        q = qk_ref[:, ksl] * (GLA_DK ** -0.5)
        k = qk_ref[:, GLA_HK + h * GLA_DK:GLA_HK + (h + 1) * GLA_DK]
        v = v_ref[:, vsl]
        gk = gk_ref[:, ksl]
        S = s0_ref[0, 0, h]

        cum = _cumsum_rows(gk)
        last = cum[T - 1:T, :]
        o = _dot((q * jnp.exp(cum)).astype(BF16), S.astype(BF16))
        for d in range(T):
            k_d = k if d == 0 else pltpu.roll(k, d, 0)
            cum_d = cum if d == 0 else pltpu.roll(cum, d, 0)
            v_d = v if d == 0 else pltpu.roll(v, d, 0)
            w = jnp.sum(q * k_d * jnp.exp(jnp.minimum(cum - cum_d, 0.0)), axis=-1, keepdims=True)
            o = o + jnp.where(row >= d, w * v_d, 0.0)

        kd = k * jnp.exp(last - cum)
        stacked = jnp.concatenate(
            [kd, jnp.broadcast_to(jnp.exp(last), (T, GLA_DK)), jnp.zeros((128 - 2 * T, GLA_DK), F32)], axis=0)
        st = stacked.T
        vpad = jnp.concatenate([v, jnp.zeros((128 - T, GLA_DV), F32)], axis=0)
        lane = lax.broadcasted_iota(jnp.int32, (GLA_DK, 128), 1)
        kdt = jnp.where(lane < T, st, 0.0).astype(BF16)
        s_ref[0, 0, h] = st[:, T:T + 1] * S + _dot(kdt, vpad.astype(BF16))

        outs.append(_head_rmsnorm_gate(o, g_ref[...], og_ref[:, vsl]))
    a_ref[...] = jnp.concatenate(outs, axis=1).astype(BF16)


def gla_sample(proj, og, gk, onorm_g, state_all, layer, states_out=None):
    r0 = NP_TOK // DEC_SEQ
    n_a = state_all.shape[0]
    st_block = (1, 1, GLA_H, GLA_DK, GLA_DV)
    in_specs = [
        pl.BlockSpec((DEC_SEQ, 2 * GLA_HK), lambda b: (r0 + b, 0)),
        pl.BlockSpec((DEC_SEQ, GLA_HV), lambda b: (r0 + b, 1)),
        pl.BlockSpec((DEC_SEQ, GLA_HV), lambda b: (r0 + b, 0)),
        pl.BlockSpec((DEC_SEQ, GLA_HK), lambda b: (r0 + b, 0)),
        pl.BlockSpec((1, GLA_DV), lambda b: (0, 0)),
        pl.BlockSpec(st_block, lambda b: (layer, b, 0, 0, 0)),
    ]
    args = [proj, proj, og, gk, onorm_g.reshape(1, GLA_DV), state_all]
    aliases = {}
    if states_out is not None:
        in_specs.append(pl.BlockSpec(memory_space=pl.ANY))
        args.append(states_out)
        aliases = {len(args) - 1: 1}
    return pl.pallas_call(
        functools.partial(_gla_sample_kernel, aliased=states_out is not None),
        grid=(DEC_BATCH,),
        in_specs=in_specs,
        out_specs=[
            pl.BlockSpec((DEC_SEQ, GLA_HV), lambda b: (b, 0)),
            pl.BlockSpec(st_block, lambda b: (layer, b, 0, 0, 0)),
        ],
        out_shape=[
            jax.ShapeDtypeStruct((NS_TOK, GLA_HV), BF16),
            jax.ShapeDtypeStruct((n_a, DEC_BATCH, GLA_H, GLA_DK, GLA_DV), F32),
        ],
        input_output_aliases=aliases,
        compiler_params=_params(("arbitrary",)),
        name="gla_sample",
    )(*args)


def _rope_partner(x):
    lane = lax.broadcasted_iota(jnp.int32, x.shape, 1)
    return jnp.where((lane & (MLA_ROPE - 1)) < MLA_ROPE // 2, pltpu.roll(x, x.shape[1] - MLA_ROPE // 2, 1),
                     pltpu.roll(x, MLA_ROPE // 2, 1))


def _kvprep_kernel(ckv_ref, kr_ref, g_ref, ckvn_ref, krr_ref, c4_ref, s4_ref, *, np_tiles):
    i = pl.program_id(0)
    x = ckv_ref[...]
    ms = jnp.mean(x * x, axis=-1, keepdims=True)
    ckvn_ref[...] = x * lax.rsqrt(ms + EPS) * g_ref[...]

    r = i * TM + lax.broadcasted_iota(jnp.int32, (TM, 128), 0)
    pos = jnp.where(i < np_tiles, r & (SEQ - 1), PAST + (r & (DEC_SEQ - 1))).astype(F32)
    k2 = (lax.broadcasted_iota(jnp.int32, (8, 128), 1) & (MLA_ROPE // 2 - 1)) * 2
    inv = jnp.power(jnp.float32(ROPE_BASE), -k2.astype(F32) / MLA_ROPE)[0:1, :]
    ang = pos * inv
    lane = lax.broadcasted_iota(jnp.int32, (TM, 128), 1)
    c4 = jnp.cos(ang)
    s4 = jnp.where((lane & (MLA_ROPE - 1)) < MLA_ROPE // 2, -jnp.sin(ang), jnp.sin(ang))
    c4_ref[...] = c4
    s4_ref[...] = s4
    kr = kr_ref[...]
    krr_ref[...] = (kr * c4 + _rope_partner(kr) * s4)[:, :MLA_ROPE]


def mla_kvprep(proj, kv_norm_g):
    np_tiles = NP_TOK // TM
    return pl.pallas_call(
        functools.partial(_kvprep_kernel, np_tiles=np_tiles),
        grid=(N_TOK // TM,),
        in_specs=[
            pl.BlockSpec((TM, MLA_RANK), lambda i: (i, 3072 // MLA_RANK)),
            pl.BlockSpec((TM, 128), lambda i: (i, 5632 // 128)),
            pl.BlockSpec((1, MLA_RANK), lambda i: (0, 0)),
        ],
        out_specs=[
            pl.BlockSpec((TM, MLA_RANK), lambda i: (i, 0)),
            pl.BlockSpec((TM, MLA_ROPE), lambda i: (i, 0)),
            pl.BlockSpec((TM, 128), lambda i: (i, 0)),
            pl.BlockSpec((TM, 128), lambda i: (i, 0)),
        ],
        out_shape=[
            jax.ShapeDtypeStruct((N_TOK, MLA_RANK), F32),
            jax.ShapeDtypeStruct((N_TOK, MLA_ROPE), F32),
            jax.ShapeDtypeStruct((N_TOK, 128), F32),
            jax.ShapeDtypeStruct((N_TOK, 128), F32),
        ],
        compiler_params=_params(("arbitrary",)),
        name="mla_kvprep",
    )(proj, proj, kv_norm_g.reshape(1, MLA_RANK))


def _kvup_kernel(ckv_ref, kr_ref, wuk_ref, wuv_ref, k_ref, v_ref):
    c = ckv_ref[...].astype(BF16)
    kn = _dot(c, wuk_ref[...]).astype(BF16)
    vv = _dot(c, wuv_ref[...]).astype(BF16)
    kr = kr_ref[...].astype(BF16)
    for h in range(MLA_H):
        k_ref[h, :, :MLA_NOPE] = kn[:, h * MLA_NOPE:(h + 1) * MLA_NOPE]
        k_ref[h, :, MLA_NOPE:] = kr
        v_ref[h] = vv[:, h * MLA_VH:(h + 1) * MLA_VH]


def mla_kvup(ckv_n, kr_r, w_uk2, w_uv2):
    return pl.pallas_call(
        _kvup_kernel,
        grid=(NP_TOK // TM,),
        in_specs=[
            pl.BlockSpec((TM, MLA_RANK), lambda i: (i, 0)),
            pl.BlockSpec((TM, MLA_ROPE), lambda i: (i, 0)),
            pl.BlockSpec((MLA_RANK, MLA_H * MLA_NOPE), lambda i: (0, 0)),
            pl.BlockSpec((MLA_RANK, MLA_H * MLA_VH), lambda i: (0, 0)),
        ],
        out_specs=[
            pl.BlockSpec((MLA_H, TM, MLA_QK), lambda i: (0, i, 0)),
            pl.BlockSpec((MLA_H, TM, MLA_VH), lambda i: (0, i, 0)),
        ],
        out_shape=[
            jax.ShapeDtypeStruct((MLA_H, NP_TOK, MLA_QK), BF16),
            jax.ShapeDtypeStruct((MLA_H, NP_TOK, MLA_VH), BF16),
        ],
        compiler_params=_params(("arbitrary",)),
        name="mla_kvup",
    )(ckv_n, kr_r, w_uk2, w_uv2)


FLASH_T = 512


def _rope_pair_select(x2, h):
    return jnp.where(h % 2 == 1, pltpu.roll(x2, MLA_ROPE, 1), x2)[:, :MLA_ROPE]


FLASH_G = 4


def _lanes(x, n):
    return x if n == 128 else jnp.concatenate([x] * (n // 128), axis=1)


def _flash_kernel(qt_ref, kt_ref, qn_ref, qr_ref, c4_ref, s4_ref, k_ref, v_ref, gate_ref, o_ref,
                  q_sc, m_sc, l_sc, acc_sc):
    s_id = pl.program_id(2)
    qi = qt_ref[s_id]
    ki = kt_ref[s_id]
    T, G = FLASH_T, FLASH_G

    @pl.when(ki == 0)
    def _():
        x = qr_ref[...]
        c = _lanes(c4_ref[...], G * MLA_ROPE)
        s = _lanes(s4_ref[...], G * MLA_ROPE)
        rot = (x * c + _rope_partner(x) * s) * MLA_SCALE
        for g in range(G):
            q_sc[g, :, :MLA_NOPE] = (qn_ref[:, g * MLA_NOPE:(g + 1) * MLA_NOPE] * MLA_SCALE).astype(BF16)
            q_sc[g, :, MLA_NOPE:] = rot[:, g * MLA_ROPE:(g + 1) * MLA_ROPE].astype(BF16)
        m_sc[...] = jnp.full_like(m_sc, _NEG)
        l_sc[...] = jnp.zeros_like(l_sc)
        acc_sc[...] = jnp.zeros_like(acc_sc)

    def step(masked):
        for g in range(G):
            s = _dot_nt(q_sc[g], k_ref[g])
            if masked:
                r = lax.broadcasted_iota(jnp.int32, (T, T), 0)
                c = lax.broadcasted_iota(jnp.int32, (T, T), 1)
                s = jnp.where(c <= r, s, _NEG)
            m_prev = m_sc[g]
            m_new = jnp.maximum(m_prev, jnp.max(s, axis=-1, keepdims=True))
            alpha = jnp.exp(m_prev - m_new)
            p = jnp.exp(s - _lanes(m_new, T))
            l_sc[g] = alpha * l_sc[g] + jnp.sum(p, axis=-1, keepdims=True)
            acc_sc[g] = alpha * acc_sc[g] + _dot(p.astype(BF16), v_ref[g])
            m_sc[g] = m_new

    @pl.when(ki < qi)
    def _():
        step(False)

    @pl.when(ki == qi)
    def _():
        step(True)
        for g in range(G):
            sl = slice(g * MLA_VH, (g + 1) * MLA_VH)
            o = acc_sc[g] / l_sc[g]
            o_ref[:, sl] = (o * _silu(gate_ref[:, sl])).astype(BF16)


def mla_flash(proj, c4, s4, kcat, vv):
    T, G = FLASH_T, FLASH_G
    nq = SEQ // T
    pairs = [(qi, ki) for qi in range(nq) for ki in range(qi + 1)]
    qt = jnp.asarray(np.array([p[0] for p in pairs], np.int32))
    kt = jnp.asarray(np.array([p[1] for p in pairs], np.int32))

    def qrow(b, g, s, qt, kt):
        return b * nq + qt[s]

    def krow(b, g, s, qt, kt):
        return b * nq + kt[s]

    grid_spec = pltpu.PrefetchScalarGridSpec(
        num_scalar_prefetch=2,
        grid=(BATCH, MLA_H // G, len(pairs)),
        in_specs=[
            pl.BlockSpec((T, G * MLA_NOPE), lambda b, g, s, qt, kt: (qrow(b, g, s, qt, kt), g)),
            pl.BlockSpec((T, G * MLA_ROPE),
                         lambda b, g, s, qt, kt: (qrow(b, g, s, qt, kt), 2048 // (G * MLA_ROPE) + g)),
            pl.BlockSpec((T, 128), lambda b, g, s, qt, kt: (qrow(b, g, s, qt, kt), 0)),
            pl.BlockSpec((T, 128), lambda b, g, s, qt, kt: (qrow(b, g, s, qt, kt), 0)),
            pl.BlockSpec((G, T, MLA_QK), lambda b, g, s, qt, kt: (g, krow(b, g, s, qt, kt), 0)),
            pl.BlockSpec((G, T, MLA_VH), lambda b, g, s, qt, kt: (g, krow(b, g, s, qt, kt), 0)),
            pl.BlockSpec((T, G * MLA_VH),
                         lambda b, g, s, qt, kt: (qrow(b, g, s, qt, kt), 3584 // (G * MLA_VH) + g)),
        ],
        out_specs=pl.BlockSpec((T, G * MLA_VH), lambda b, g, s, qt, kt: (qrow(b, g, s, qt, kt), g)),
        scratch_shapes=[
            pltpu.VMEM((G, T, MLA_QK), BF16),
            pltpu.VMEM((G, T, 128), F32),
            pltpu.VMEM((G, T, 128), F32),
            pltpu.VMEM((G, T, MLA_VH), F32),
        ],
    )
    return pl.pallas_call(
        _flash_kernel,
        grid_spec=grid_spec,
        out_shape=jax.ShapeDtypeStruct((NP_TOK, MLA_H * MLA_VH), BF16),
        compiler_params=_params(("arbitrary", "arbitrary", "arbitrary")),
        name="mla_flash",
    )(qt, kt, proj, proj, c4, s4, kcat, vv, proj)


def _qabs_kernel(qn_ref, qr_ref, c4_ref, s4_ref, wukt_ref, o_ref):
    h = pl.program_id(0)
    ql = _dot(qn_ref[...].astype(BF16), wukt_ref[0]) * MLA_SCALE
    x2 = qr_ref[...]
    rot = x2 * c4_ref[...] + _rope_partner(x2) * s4_ref[...]
    qr = _rope_pair_select(rot, h) * MLA_SCALE
    o_ref[:, 0, :, :MLA_RANK] = ql.reshape(DEC_BATCH, DEC_SEQ, MLA_RANK)
    o_ref[:, 0, :, MLA_RANK:] = qr.reshape(DEC_BATCH, DEC_SEQ, MLA_ROPE)


def mla_qabs(proj, c4, s4, w_ukt3):
    rb = NP_TOK // NS_TOK
    return pl.pallas_call(
        _qabs_kernel,
        grid=(MLA_H,),
        in_specs=[
            pl.BlockSpec((NS_TOK, 128), lambda h: (rb, h)),
            pl.BlockSpec((NS_TOK, 128), lambda h: (rb, 2048 // 128 + h // 2)),
            pl.BlockSpec((NS_TOK, 128), lambda h: (rb, 0)),
            pl.BlockSpec((NS_TOK, 128), lambda h: (rb, 0)),
            pl.BlockSpec((1, MLA_NOPE, MLA_RANK), lambda h: (h, 0, 0)),
        ],
        out_specs=pl.BlockSpec((DEC_BATCH, 1, DEC_SEQ, MLA_LAT), lambda h: (0, h, 0, 0)),
        out_shape=jax.ShapeDtypeStruct((DEC_BATCH, MLA_H, DEC_SEQ, MLA_LAT), F32),
        compiler_params=_params(("arbitrary",)),
        name="mla_qabs",
    )(proj, proj, c4, s4, w_ukt3)


DEC_SEQS = 2
DEC_PG = 16
DEC_NG = N_PAGES // DEC_PG
DEC_SLOTS = 3
DEC_AHEAD = DEC_SLOTS - 1


def _decode_kernel(pt_ref, q_ref, cnew_ref, knew_ref, ckv_hbm, krt_hbm, o_ref,
                   kv_buf, kr_buf, sem, m_sc, l_sc, acc_sc, *, layer):
    step = pl.program_id(0)
    n_groups = pl.num_programs(0) * DEC_NG
    R = MLA_H * DEC_SEQ

    def group_copies(n, slot):
        first_page = lax.div(n, DEC_NG) * (DEC_SEQS * N_PAGES) + lax.rem(n, DEC_NG) * DEC_PG
        cps = []
        for s in range(DEC_SEQS):
            for r in range(DEC_PG):
                page = pt_ref[first_page + s * N_PAGES + r]
                j = s * DEC_PG + r
                cps.append(pltpu.make_async_copy(ckv_hbm.at[layer, page], kv_buf.at[slot, j], sem.at[0, slot]))
                cps.append(pltpu.make_async_copy(krt_hbm.at[layer, page], kr_buf.at[slot, j], sem.at[1, slot]))
        return cps

    def start_group(n, slot):
        for i, cp in enumerate(group_copies(n, slot)):
            cp.start(priority=(i // 2) % 2)

    @pl.when(step == 0)
    def _():
        for n0 in range(DEC_AHEAD):
            start_group(n0, n0)

    m_sc[...] = jnp.full_like(m_sc, _NEG)
    l_sc[...] = jnp.zeros_like(l_sc)
    acc_sc[...] = jnp.zeros_like(acc_sc)

    ql, qr = [], []
    for s in range(DEC_SEQS):
        q = q_ref[s].reshape(R, MLA_LAT)
        ql.append(q[:, :MLA_RANK].astype(BF16))
        qr.append(q[:, MLA_RANK:].astype(BF16))

    def online(s, scores, vals):
        m_prev = m_sc[s]
        m_new = jnp.maximum(m_prev, jnp.max(scores, axis=-1, keepdims=True))
        alpha = jnp.exp(m_prev - m_new)
        p = jnp.exp(scores - _lanes(m_new, scores.shape[1]))
        l_sc[s] = alpha * l_sc[s] + jnp.sum(p, axis=-1, keepdims=True)
        acc_sc[s] = _lanes(alpha, MLA_RANK) * acc_sc[s] + _dot(p.astype(BF16), vals)
        m_sc[s] = m_new

    def group(g, carry):
        n = step * DEC_NG + g
        slot = lax.rem(n, DEC_SLOTS)
        for cp in group_copies(n, slot):
            cp.wait()

        @pl.when(n + DEC_AHEAD < n_groups)
        def _():
            start_group(n + DEC_AHEAD, lax.rem(n + DEC_AHEAD, DEC_SLOTS))

        for s in range(DEC_SEQS):
            kv = jnp.concatenate(
                [kv_buf[slot, s * DEC_PG + r].astype(BF16) for r in range(DEC_PG)], axis=0)
            krt = jnp.concatenate(
                [kr_buf[slot, s * DEC_PG + r].astype(BF16) for r in range(DEC_PG)], axis=1)
            online(s, _dot_nt(ql[s], kv) + _dot(qr[s], krt), kv)
        return carry

    lax.fori_loop(0, DEC_NG, group, 0)

    t = lax.broadcasted_iota(jnp.int32, (R, PAGE), 0) & (DEC_SEQ - 1)
    j = lax.broadcasted_iota(jnp.int32, (R, PAGE), 1)
    for s in range(DEC_SEQS):
        rows = slice(s * DEC_SEQ, (s + 1) * DEC_SEQ)
        cn = jnp.concatenate([cnew_ref[rows, :], jnp.zeros((PAGE - DEC_SEQ, MLA_RANK), F32)], axis=0).astype(BF16)
        kn = jnp.concatenate([knew_ref[rows, :], jnp.zeros((PAGE - DEC_SEQ, MLA_ROPE), F32)], axis=0).astype(BF16)
        sc = _dot_nt(ql[s], cn) + _dot_nt(qr[s], kn)
        online(s, jnp.where(j <= t, sc, _NEG), cn)
        o = acc_sc[s] / _lanes(l_sc[s], MLA_RANK)
        o_ref[s] = o.reshape(MLA_H, DEC_SEQ, MLA_RANK)


def mla_decode(qcat, cache_ckv, cache_krt, layer, page_table, ckv_n, kr_r):
    rows = DEC_SEQS * DEC_SEQ
    r0 = NP_TOK // rows
    n_pg = DEC_SEQS * DEC_PG
    R = MLA_H * DEC_SEQ
    grid_spec = pltpu.PrefetchScalarGridSpec(
        num_scalar_prefetch=1,
        grid=(DEC_BATCH // DEC_SEQS,),
        in_specs=[
            pl.BlockSpec((DEC_SEQS, MLA_H, DEC_SEQ, MLA_LAT), lambda i, pt: (i, 0, 0, 0)),
            pl.BlockSpec((rows, MLA_RANK), lambda i, pt: (r0 + i, 0)),
            pl.BlockSpec((rows, MLA_ROPE), lambda i, pt: (r0 + i, 0)),
            pl.BlockSpec(memory_space=pl.ANY),
            pl.BlockSpec(memory_space=pl.ANY),
        ],
        out_specs=pl.BlockSpec((DEC_SEQS, MLA_H, DEC_SEQ, MLA_RANK), lambda i, pt: (i, 0, 0, 0)),
        scratch_shapes=[
            pltpu.VMEM((DEC_SLOTS, n_pg, PAGE, MLA_RANK), F32),
            pltpu.VMEM((DEC_SLOTS, n_pg, MLA_ROPE, PAGE), F32),
            pltpu.SemaphoreType.DMA((2, DEC_SLOTS)),
            pltpu.VMEM((DEC_SEQS, R, 128), F32),
            pltpu.VMEM((DEC_SEQS, R, 128), F32),
            pltpu.VMEM((DEC_SEQS, R, MLA_RANK), F32),
        ],
    )
    return pl.pallas_call(
        functools.partial(_decode_kernel, layer=layer),
        grid_spec=grid_spec,
        out_shape=jax.ShapeDtypeStruct((DEC_BATCH, MLA_H, DEC_SEQ, MLA_RANK), F32),
        compiler_params=_params(("arbitrary",)),
        name="mla_decode",
    )(page_table.reshape(-1), qcat, ckv_n, kr_r, cache_ckv, cache_krt)


def _uvup_kernel(ol_ref, wuv_ref, gate_ref, a_ref):
    ol = ol_ref[...].reshape(NS_TOK, MLA_RANK).astype(BF16)
    a_ref[...] = (_dot(ol, wuv_ref[0]) * _silu(gate_ref[...])).astype(BF16)


def mla_uvup(o_lat, w_uv3, proj):
    rb = NP_TOK // NS_TOK
    return pl.pallas_call(
        _uvup_kernel,
        grid=(MLA_H,),
        in_specs=[
            pl.BlockSpec((DEC_BATCH, 1, DEC_SEQ, MLA_RANK), lambda h: (0, h, 0, 0)),
            pl.BlockSpec((1, MLA_RANK, MLA_VH), lambda h: (h, 0, 0)),
            pl.BlockSpec((NS_TOK, 128), lambda h: (rb, 3584 // 128 + h)),
        ],
        out_specs=pl.BlockSpec((NS_TOK, MLA_VH), lambda h: (0, h)),
        out_shape=jax.ShapeDtypeStruct((NS_TOK, MLA_H * MLA_VH), BF16),
        compiler_params=_params(("arbitrary",)),
        name="mla_uvup",
    )(o_lat, w_uv3, proj)


POOL_HALO = 16


def _pool_prompt_kernel(u_ref, halo_ref, gate_ref, wg_ref, sc_ref, a_ref, tail_ref, ext_sc, *, tiles_per_seq):
    i = pl.program_id(0)
    first = (i % tiles_per_seq) == 0
    tail_ref[0] = u_ref[TM - POOL_HALO:, :]
    ext_sc[:POOL_HALO, :] = jnp.where(first, 0.0, halo_ref[...])
    ext_sc[POOL_HALO:, :] = u_ref[...]
    t = (i % tiles_per_seq) * TM + lax.broadcasted_iota(jnp.int32, (TM, 1), 0)

    for gi, w in enumerate(POOL_WINDOWS):
        cols = slice(gi * POOL_G, (gi + 1) * POOL_G)
        u = u_ref[:, cols]
        acc = u
        for j in range(1, w):
            acc = acc + ext_sc[POOL_HALO - j:POOL_HALO - j + TM, cols]
        cnt = jnp.minimum(t + 1, w).astype(F32)
        p = (acc / cnt - u).astype(BF16)
        z = _dot(p, wg_ref[gi]) * sc_ref[:, cols]
        a_ref[:, cols] = (z * _silu(gate_ref[:, cols])).astype(BF16)


def pool_prompt(proj, w_grp, pscale):
    tiles_per_seq = SEQ // TM
    hb = TM // POOL_HALO
    ng = len(POOL_WINDOWS)
    return pl.pallas_call(
        functools.partial(_pool_prompt_kernel, tiles_per_seq=tiles_per_seq),
        grid=(NP_TOK // TM,),
        in_specs=[
            pl.BlockSpec((TM, D), lambda i: (i, 0)),
            pl.BlockSpec((POOL_HALO, D), lambda i: (jnp.maximum(i * hb - 1, 0), 0)),
            pl.BlockSpec((TM, D), lambda i: (i, 1)),
            pl.BlockSpec((ng, POOL_G, POOL_G), lambda i: (0, 0, 0)),
            pl.BlockSpec((1, D), lambda i: (0, 0)),
        ],
        out_specs=[
            pl.BlockSpec((TM, D), lambda i: (i, 0)),
            pl.BlockSpec((1, POOL_HALO, D), lambda i: (i, 0, 0)),
        ],
        out_shape=[
            jax.ShapeDtypeStruct((NP_TOK, D), BF16),
            jax.ShapeDtypeStruct((NP_TOK // TM, POOL_HALO, D), F32),
        ],
        scratch_shapes=[pltpu.VMEM((POOL_HALO + TM, D), F32)],
        compiler_params=_params(("arbitrary",)),
        name="pool_prompt",
    )(proj, proj, proj, w_grp, pscale)


def _pool_sample_kernel(u_ref, gate_ref, hist_ref, wg_ref, sc_ref, a_ref):
    g = pl.program_id(0)

    def seq(r):
        return hist_ref[r] if r < POOL_HIST else u_ref[r - POOL_HIST]

    for gi, w in enumerate(POOL_WINDOWS):
        @pl.when(g == gi)
        def _(w=w):
            ps = []
            for t in range(DEC_SEQ):
                acc = seq(POOL_HIST + t)
                for j in range(1, w):
                    acc = acc + seq(POOL_HIST + t - j)
                ps.append(acc / float(w) - u_ref[t])
            p = jnp.concatenate(ps, axis=0).astype(BF16)
            z = _dot(p, wg_ref[0]) * sc_ref[...]
            gate = gate_ref[...].reshape(NS_TOK, POOL_G)
            a_ref[...] = (z * _silu(gate)).astype(BF16).reshape(DEC_SEQ, DEC_BATCH, POOL_G)


def pool_sample(proj_t, hist_t, w_grp, pscale):
    ng = len(POOL_WINDOWS)
    return pl.pallas_call(
        _pool_sample_kernel,
        grid=(ng,),
        in_specs=[
            pl.BlockSpec((DEC_SEQ, DEC_BATCH, POOL_G), lambda g: (0, 0, g)),
            pl.BlockSpec((DEC_SEQ, DEC_BATCH, POOL_G), lambda g: (0, 0, ng + g)),
            pl.BlockSpec((POOL_HIST, DEC_BATCH, POOL_G), lambda g: (0, 0, g)),
            pl.BlockSpec((1, POOL_G, POOL_G), lambda g: (g, 0, 0)),
            pl.BlockSpec((1, POOL_G), lambda g: (0, g)),
        ],
        out_specs=pl.BlockSpec((DEC_SEQ, DEC_BATCH, POOL_G), lambda g: (0, 0, g)),
        out_shape=jax.ShapeDtypeStruct((DEC_SEQ, DEC_BATCH, D), BF16),
        compiler_params=_params(("arbitrary",)),
        name="pool_sample",
    )(proj_t, proj_t, hist_t, w_grp, pscale)


def kernel(x_prompt, x_sample, c_prompt, c_sample, state_gla, cache_ckv, cache_kr, state_pool, page_table, norm_g, ada_w, ada_b, final_norm_g, gla_w_in, gla_w_gate_up, gla_b_gate, gla_onorm_g, gla_w_out, mla_w_in, mla_kv_norm_g, mla_w_uk, mla_w_uv, mla_w_out, pool_w_in, pool_w_grp, pool_scale, pool_w_out):
    x_all = (x_prompt.reshape(NP_TOK, D), x_sample.reshape(NS_TOK, D))

    n_c = BATCH + DEC_BATCH
    c_all = jnp.concatenate([c_prompt, c_sample, jnp.zeros((8 - n_c % 8, D), F32)], axis=0)
    mod = ada_mod(c_all, ada_w, ada_b)

    gla_states_p, gla_states_s = [], None
    ckv_rows, kr_rows, pool_p, pool_s = [], [], [], []
    ia = ib = ic = 0
    mods = [(mod[l, :BATCH].reshape(BATCH, 1, 3 * D), jnp.repeat(mod[l, BATCH:n_c], DEC_SEQ, axis=0))
            for l in range(DEPTH)]
    h = norm_mod(x_all, norm_g[0], *mods[0])
    for l in range(DEPTH):
        mod_p, mod_s = mods[l]
        mixer = LAYER_MIXER[l]
        if mixer == 0:
            w_t = jnp.swapaxes(gla_w_in[ia], 0, 1).astype(BF16)
            cut = 2 * GLA_HK + GLA_HV
            w_glr = jnp.pad(w_t[cut:cut + GLA_RANK], ((0, 128 - GLA_RANK), (0, 0)))
            w_up = jnp.pad(gla_w_gate_up[ia], ((0, 128 - GLA_RANK), (0, 0))).astype(BF16)
            proj = proj_matmul(h, w_t, 1024, True, n=cut)
            og = proj_matmul(h, w_t[cut + GLA_RANK:], 1024, True)
            gk = gla_gate(h, w_glr, w_up, gla_b_gate[ia].reshape(1, GLA_HK))
            a_p, st_p = gla_prompt(proj, og, gk, gla_onorm_g[ia])
            a_s, gla_states_s = gla_sample(proj, og, gk, gla_onorm_g[ia], state_gla, ia, states_out=gla_states_s)
            gla_states_p.append(st_p)
            w_out = gla_w_out[ia].astype(BF16)
            ia += 1
        elif mixer == 1:
            w_t = jnp.swapaxes(mla_w_in[ib], 0, 1)
            nq = MLA_H * MLA_QK
            wq = w_t[:nq].reshape(MLA_H, MLA_QK, D)
            w_perm = jnp.concatenate([
                wq[:, :MLA_NOPE].reshape(MLA_H * MLA_NOPE, D),
                wq[:, MLA_NOPE:].reshape(MLA_H * MLA_ROPE, D),
                w_t[nq:nq + MLA_RANK],
                w_t[nq + MLA_RANK + MLA_ROPE:],
                w_t[nq + MLA_RANK:nq + MLA_RANK + MLA_ROPE],
                jnp.zeros((128 - MLA_ROPE, D), F32),
            ], axis=0).astype(BF16)
            proj = proj_matmul(h, w_perm, 1152, True)
            ckv_n, kr_r, c4, s4 = mla_kvprep(proj, mla_kv_norm_g[ib])
            w_uk2 = mla_w_uk[ib].reshape(MLA_RANK, MLA_H * MLA_NOPE).astype(BF16)
            w_uv2 = mla_w_uv[ib].reshape(MLA_RANK, MLA_H * MLA_VH).astype(BF16)
            w_ukt3 = jnp.transpose(mla_w_uk[ib], (1, 2, 0)).astype(BF16)
            w_uv3 = jnp.transpose(mla_w_uv[ib], (1, 0, 2)).astype(BF16)
            kcat, vv = mla_kvup(ckv_n, kr_r, w_uk2, w_uv2)
            a_p = mla_flash(proj, c4, s4, kcat, vv)
            qcat = mla_qabs(proj, c4, s4, w_ukt3)
            cache_krt = jnp.swapaxes(cache_kr, 2, 3)
            o_lat = mla_decode(qcat, cache_ckv, cache_krt, ib, page_table, ckv_n, kr_r)
            a_s = mla_uvup(o_lat, w_uv3, proj)
            ckv_rows.append(ckv_n)
            kr_rows.append(kr_r)
            w_out = mla_w_out[ib].astype(BF16)
            ib += 1
        else:
            proj = proj_matmul(h, pool_w_in[ic].astype(BF16), 1024, False)
            w_grp = pool_w_grp[ic].astype(BF16)
            pscale = pool_scale[ic].reshape(1, D)
            a_p, u_tails = pool_prompt(proj, w_grp, pscale)
            proj_t = jnp.transpose(proj[NP_TOK:].reshape(DEC_BATCH, DEC_SEQ, 2 * D), (1, 0, 2))
            hist_t = jnp.transpose(state_pool[ic], (1, 0, 2))
            a_st = pool_sample(proj_t, hist_t, w_grp, pscale)
            a_s = jnp.transpose(a_st, (1, 0, 2)).reshape(NS_TOK, D)
            u_s = proj[NP_TOK:, :D].reshape(DEC_BATCH, DEC_SEQ, D)
            pool_p.append(u_tails.reshape(BATCH, SEQ // TM, POOL_HALO, D)[:, -1, POOL_HALO - POOL_HIST:, :])
            pool_s.append(jnp.concatenate([state_pool[ic][:, DEC_SEQ:, :], u_s], axis=1))
            w_out = pool_w_out[ic].astype(BF16)
            ic += 1
        if l == DEPTH - 1:
            y_p, y_s = out_proj(a_p, a_s, w_out, x_all, mod_p, mod_s, final_g=final_norm_g)
        else:
            x_all, h = out_proj(a_p, a_s, w_out, x_all, mod_p, mod_s, next_norm=(norm_g[l + 1],) + mods[l + 1])

    y_prompt = y_p.reshape(BATCH, SEQ, D)
    y_sample = y_s.reshape(DEC_BATCH, DEC_SEQ, D)
    ckv_all = jnp.stack(ckv_rows)
    kr_all = jnp.stack(kr_rows)
    return (
        y_prompt,
        y_sample,
        jnp.stack(gla_states_p),
        gla_states_s,
        ckv_all[:, :NP_TOK].reshape(-1, BATCH, SEQ, MLA_RANK),
        kr_all[:, :NP_TOK].reshape(-1, BATCH, SEQ, MLA_ROPE),
        ckv_all[:, NP_TOK:].reshape(-1, DEC_BATCH, DEC_SEQ, MLA_RANK),
        kr_all[:, NP_TOK:].reshape(-1, DEC_BATCH, DEC_SEQ, MLA_ROPE),
        jnp.stack(pool_p),
        jnp.stack(pool_s),
    )
```

```python
import functools

import numpy as np
import jax
import jax.numpy as jnp
from jax import lax
from jax.experimental import pallas as pl
from jax.experimental.pallas import tpu as pltpu

F32 = jnp.float32
BF16 = jnp.bfloat16

D = 2048
BATCH = 4
SEQ = 2048
DEC_BATCH = 128
DEC_SEQ = 8
PAGE = 128
N_PAGES = 64
PAST = N_PAGES * PAGE
DEPTH = 4
LAYER_MIXER = (0, 1, 2, 0)
EPS = 1e-6

NP_TOK = BATCH * SEQ
NS_TOK = DEC_BATCH * DEC_SEQ
N_TOK = NP_TOK + NS_TOK

GLA_H = 4
GLA_DK = 256
GLA_DV = 512
GLA_RANK = 16
GLA_TAU = 16.0
GLA_HK = GLA_H * GLA_DK
GLA_HV = GLA_H * GLA_DV
GLA_C = 128
GLA_SUB = 16

MLA_H = 16
MLA_NOPE = 128
MLA_ROPE = 64
MLA_VH = 128
MLA_RANK = 512
MLA_SCALE = (MLA_NOPE + MLA_ROPE) ** -0.5
MLA_QK = MLA_NOPE + MLA_ROPE
MLA_LAT = MLA_RANK + MLA_ROPE
MLA_N = 5760
ROPE_BASE = 10000.0

POOL_WINDOWS = (2, 4, 8, 16)
POOL_G = 512
POOL_HIST = 15

TM = 512
TM_OUT = 256
VMEM_LIMIT = 56 * 1024 * 1024

_NEG = -1e30


def _params(sem):
    return pltpu.CompilerParams(dimension_semantics=sem, vmem_limit_bytes=VMEM_LIMIT)


def _silu(x):
    return x * (1.0 / (1.0 + jnp.exp(-x)))


def _dot(a, b):
    return jnp.dot(a, b, preferred_element_type=F32)


def _dot_nt(a, b):
    return lax.dot_general(a, b, (((1,), (1,)), ((), ())), preferred_element_type=F32)


def _ada_kernel(c_ref, w_ref, b_ref, o_ref):
    sc = _silu(c_ref[...]).astype(BF16)
    o_ref[0] = _dot(sc, w_ref[0].astype(BF16)) + b_ref[0]


def ada_mod(c_all, ada_w, ada_b):
    rows = c_all.shape[0]
    tn = 768
    return pl.pallas_call(
        _ada_kernel,
        grid=(DEPTH, 3 * D // tn),
        in_specs=[
            pl.BlockSpec((rows, D), lambda l, j: (0, 0)),
            pl.BlockSpec((1, D, tn), lambda l, j: (l, 0, j)),
            pl.BlockSpec((1, 1, tn), lambda l, j: (l, 0, j)),
        ],
        out_specs=pl.BlockSpec((1, rows, tn), lambda l, j: (l, 0, j)),
        out_shape=jax.ShapeDtypeStruct((DEPTH, rows, 3 * D), F32),
        compiler_params=_params(("arbitrary", "arbitrary")),
        name="ada_mod",
    )(c_all, ada_w, ada_b.reshape(DEPTH, 1, 3 * D))


def _mod_rows(i, np_tiles, p_ref, s_ref):
    return jnp.where(i < np_tiles, p_ref[0], s_ref[...])


def _norm_mod(x, g, shift, scale):
    ms = jnp.mean(x * x, axis=-1, keepdims=True)
    return ((x * lax.rsqrt(ms + EPS) * g) * (1.0 + scale) + shift).astype(BF16)


def _mod_specs(tm, cols):
    np_tiles = NP_TOK // tm
    tiles_per_seq = SEQ // tm
    specs = []
    for col in cols:
        specs.append(pl.BlockSpec((1, 1, D), lambda i, col=col: (jnp.minimum(i // tiles_per_seq, BATCH - 1), 0, col)))
        specs.append(pl.BlockSpec((tm, D), lambda i, col=col: (jnp.maximum(i - np_tiles, 0), col)))
    return specs


def _stream_args(tm, x):
    if not isinstance(x, tuple):
        return [pl.BlockSpec((tm, D), lambda i: (i, 0))], [x]
    np_tiles = NP_TOK // tm
    specs = [
        pl.BlockSpec((tm, D), lambda i: (jnp.minimum(i, np_tiles - 1), 0)),
        pl.BlockSpec((tm, D), lambda i: (jnp.maximum(i - np_tiles, 0), 0)),
    ]
    return specs, list(x)


def _stream_tile(i, np_tiles, x_refs):
    if len(x_refs) == 1:
        return x_refs[0][...]
    return jnp.where(i < np_tiles, x_refs[0][...], x_refs[1][...])


def _norm_kernel(*refs, np_tiles, n_x):
    x_refs, (g_ref, shp_ref, shs_ref, scp_ref, scs_ref, h_ref) = refs[:n_x], refs[n_x:]
    i = pl.program_id(0)
    x = _stream_tile(i, np_tiles, x_refs)
    h_ref[...] = _norm_mod(x, g_ref[...], _mod_rows(i, np_tiles, shp_ref, shs_ref),
                           _mod_rows(i, np_tiles, scp_ref, scs_ref))


def norm_mod(x, norm_g, mod_p, mod_s):
    tm = TM
    x_specs, x_args = _stream_args(tm, x)
    return pl.pallas_call(
        functools.partial(_norm_kernel, np_tiles=NP_TOK // tm, n_x=len(x_args)),
        grid=(N_TOK // tm,),
        in_specs=x_specs + [pl.BlockSpec((1, D), lambda i: (0, 0))] + _mod_specs(tm, (0, 1)),
        out_specs=pl.BlockSpec((tm, D), lambda i: (i, 0)),
        out_shape=jax.ShapeDtypeStruct((N_TOK, D), BF16),
        compiler_params=_params(("arbitrary",)),
        name="norm_mod",
    )(*x_args, norm_g.reshape(1, D), mod_p, mod_s, mod_p, mod_s)


TMM = 1024


def _mm_kernel(h_ref, w_ref, o_ref, *, w_is_nk):
    o_ref[...] = _dot_nt(h_ref[...], w_ref[...]) if w_is_nk else _dot(h_ref[...], w_ref[...])


def proj_matmul(h, w, tn, w_is_nk, n=None):
    if n is None:
        n = w.shape[0] if w_is_nk else w.shape[1]
    w_spec = pl.BlockSpec((tn, D), lambda j, i: (j, 0)) if w_is_nk else pl.BlockSpec((D, tn), lambda j, i: (0, j))
    return pl.pallas_call(
        functools.partial(_mm_kernel, w_is_nk=w_is_nk),
        grid=(n // tn, N_TOK // TMM),
        in_specs=[pl.BlockSpec((TMM, D), lambda j, i: (i, 0)), w_spec],
        out_specs=pl.BlockSpec((TMM, tn), lambda j, i: (i, j)),
        out_shape=jax.ShapeDtypeStruct((N_TOK, n), F32),
        compiler_params=_params(("arbitrary", "arbitrary")),
        name="proj_matmul",
    )(h, w)


def _gate_kernel(h_ref, wg_ref, wu_ref, bg_ref, gk_ref):
    glr = _dot_nt(h_ref[...], wg_ref[...]).astype(BF16)
    z = _dot(glr, wu_ref[...]) + bg_ref[...]
    gk_ref[...] = (jnp.minimum(z, 0.0) - jnp.log1p(jnp.exp(-jnp.abs(z)))) / GLA_TAU


def gla_gate(h, wg, wu, bg):
    tm = TM
    return pl.pallas_call(
        _gate_kernel,
        grid=(N_TOK // tm,),
        in_specs=[
            pl.BlockSpec((tm, D), lambda i: (i, 0)),
            pl.BlockSpec(wg.shape, lambda i: (0, 0)),
            pl.BlockSpec(wu.shape, lambda i: (0, 0)),
            pl.BlockSpec(bg.shape, lambda i: (0, 0)),
        ],
        out_specs=pl.BlockSpec((tm, GLA_HK), lambda i: (i, 0)),
        out_shape=jax.ShapeDtypeStruct((N_TOK, GLA_HK), F32),
        compiler_params=_params(("arbitrary",)),
        name="gla_gate",
    )(h, wg, wu, bg)


def _out_kernel(ap_ref, as_ref, w_ref, *rest, np_tiles, final, n_x):
    x_refs, (gp_ref, gs_ref), rest = rest[:n_x], rest[n_x:n_x + 2], rest[n_x + 2:]
    i = pl.program_id(0)
    a = jnp.where(i < np_tiles, ap_ref[...], as_ref[...])
    y = _dot(a, w_ref[...])
    gate = _mod_rows(i, np_tiles, gp_ref, gs_ref)
    xn = _stream_tile(i, np_tiles, x_refs) + gate * y
    if not final:
        ng_ref, shp_ref, shs_ref, scp_ref, scs_ref, o_ref, h_ref = rest
        o_ref[...] = xn
        h_ref[...] = _norm_mod(xn, ng_ref[...], _mod_rows(i, np_tiles, shp_ref, shs_ref),
                               _mod_rows(i, np_tiles, scp_ref, scs_ref))
        return
    fg_ref, yp_ref, ys_ref = rest
    ms = jnp.mean(xn * xn, axis=-1, keepdims=True)
    yn = xn * lax.rsqrt(ms + EPS) * fg_ref[...]

    @pl.when(i < np_tiles)
    def _():
        yp_ref[...] = yn

    @pl.when(i >= np_tiles)
    def _():
        ys_ref[...] = yn


def out_proj(a_p, a_s, w_out, x, mod_p, mod_s, final_g=None, next_norm=None):
    tm = TM_OUT
    np_tiles = NP_TOK // tm
    final = final_g is not None
    x_specs, x_args = _stream_args(tm, x)
    in_specs = [
        pl.BlockSpec((tm, D), lambda i: (jnp.minimum(i, np_tiles - 1), 0)),
        pl.BlockSpec((tm, D), lambda i: (jnp.maximum(i - np_tiles, 0), 0)),
        pl.BlockSpec((D, D), lambda i: (0, 0)),
    ] + x_specs + _mod_specs(tm, (2,))
    args = [a_p, a_s, w_out] + x_args + [mod_p, mod_s]
    if not final:
        ng, nmod_p, nmod_s = next_norm
        in_specs += [pl.BlockSpec((1, D), lambda i: (0, 0))] + _mod_specs(tm, (0, 1))
        args += [ng.reshape(1, D), nmod_p, nmod_s, nmod_p, nmod_s]
        out_specs = [pl.BlockSpec((tm, D), lambda i: (i, 0)), pl.BlockSpec((tm, D), lambda i: (i, 0))]
        out_shape = [jax.ShapeDtypeStruct((N_TOK, D), F32), jax.ShapeDtypeStruct((N_TOK, D), BF16)]
    else:
        in_specs.append(pl.BlockSpec((1, D), lambda i: (0, 0)))
        args.append(final_g.reshape(1, D))
        out_specs = [
            pl.BlockSpec((tm, D), lambda i: (jnp.minimum(i, np_tiles - 1), 0)),
            pl.BlockSpec((tm, D), lambda i: (jnp.maximum(i - np_tiles, 0), 0)),
        ]
        out_shape = [jax.ShapeDtypeStruct((NP_TOK, D), F32), jax.ShapeDtypeStruct((NS_TOK, D), F32)]
    return pl.pallas_call(
        functools.partial(_out_kernel, np_tiles=np_tiles, final=final, n_x=len(x_args)),
        grid=(N_TOK // tm,),
        in_specs=in_specs,
        out_specs=out_specs,
        out_shape=out_shape,
        compiler_params=_params(("arbitrary",)),
        name="out_proj_final" if final else "out_proj",
    )(*args)


def _cumsum_rows(x):
    n = x.shape[0]
    row = lax.broadcasted_iota(jnp.int32, x.shape, 0)
    s = 1
    while s < n:
        x = x + jnp.where(row >= s, pltpu.roll(x, s, 0), 0.0)
        s *= 2
    return x


LOG2E = 1.4426950408889634


def _cumsum_mxu(x):
    n = x.shape[0]
    r = lax.broadcasted_iota(jnp.int32, (n, n), 0)
    c = lax.broadcasted_iota(jnp.int32, (n, n), 1)
    tri = jnp.where(c <= r, 1.0, 0.0).astype(BF16)
    hi = x.astype(BF16)
    rem = x - hi.astype(F32)
    mid = rem.astype(BF16)
    lo = (rem - mid.astype(F32)).astype(BF16)
    return _dot(tri, hi) + _dot(tri, mid) + _dot(tri, lo)


def _col_from_row(v):
    return jnp.broadcast_to(v, (128, v.shape[1])).T


def _head_rmsnorm_gate(o, g, og):
    ms = jnp.mean(o * o, axis=-1, keepdims=True)
    return (o * lax.rsqrt(ms + EPS) * g) * _silu(og)


def _gla_chunk(q, k, v, gk, S):
    C, SUB = GLA_C, GLA_SUB
    nsub = C // SUB

    cum = _cumsum_mxu(gk) * LOG2E
    excl = cum - gk * LOG2E
    last = cum[C - 1:C, :]

    o = _dot((q * jnp.exp2(cum)).astype(BF16), S.astype(BF16))

    row_blocks = [jnp.zeros((SUB, C), F32)]
    for i in range(1, nsub):
        sl = slice(i * SUB, (i + 1) * SUB)
        n_k = i * SUB
        b_i = excl[n_k:n_k + 1, :]
        qt = (q[sl] * jnp.exp2(cum[sl] - b_i)).astype(BF16)
        kh = (k[:n_k] * jnp.exp2(b_i - cum[:n_k])).astype(BF16)
        kh = jnp.concatenate([kh, jnp.zeros((C - n_k, GLA_DK), BF16)], axis=0)
        row_blocks.append(_dot_nt(qt, kh))
    att = jnp.concatenate(row_blocks, axis=0)

    parts = []
    for d in range(SUB):
        k_d = k if d == 0 else pltpu.roll(k, d, 0)
        cum_d = cum if d == 0 else pltpu.roll(cum, d, 0)
        parts.append((q * k_d * jnp.exp2(cum - cum_d)).astype(BF16))
    ones = jnp.ones((GLA_DK, 128), BF16)
    band = _dot(jnp.concatenate(parts, axis=0), ones)
    r = lax.broadcasted_iota(jnp.int32, (C, C), 0)
    cc = lax.broadcasted_iota(jnp.int32, (C, C), 1)
    off = jnp.where((r & -SUB) == (cc & -SUB), r - cc, -1)
    for d in range(SUB):
        att = jnp.where(off == d, band[d * C:(d + 1) * C], att)

    o = o + _dot(att.astype(BF16), v)

    kd = (k * jnp.exp2(last - cum)).astype(BF16)
    dec = _col_from_row(jnp.exp2(last))
    dec = jnp.concatenate([dec] * (GLA_DV // 128), axis=1)
    return o, dec * S + _dot(kd.T, v)


def _gla_prompt_kernel(qk_ref, v_ref, og_ref, gk_ref, g_ref, a_ref, s_ref):
    @pl.when(pl.program_id(1) == 0)
    def _():
        s_ref[...] = jnp.zeros_like(s_ref)

    for h in range(GLA_H):
        ksl = slice(h * GLA_DK, (h + 1) * GLA_DK)
        vsl = slice(h * GLA_DV, (h + 1) * GLA_DV)
        q = qk_ref[:, ksl] * (GLA_DK ** -0.5)
        k = qk_ref[:, GLA_HK + h * GLA_DK:GLA_HK + (h + 1) * GLA_DK]
        o, s_new = _gla_chunk(q, k, v_ref[:, vsl].astype(BF16), gk_ref[:, ksl], s_ref[0, h])
        s_ref[0, h] = s_new
        a_ref[:, vsl] = _head_rmsnorm_gate(o, g_ref[...], og_ref[:, vsl]).astype(BF16)


def gla_prompt(proj, og, gk, onorm_g):
    nc = SEQ // GLA_C

    def rows(b, c):
        return b * nc + c

    return pl.pallas_call(
        _gla_prompt_kernel,
        grid=(BATCH, nc),
        in_specs=[
            pl.BlockSpec((GLA_C, 2 * GLA_HK), lambda b, c: (rows(b, c), 0)),
            pl.BlockSpec((GLA_C, GLA_HV), lambda b, c: (rows(b, c), 1)),
            pl.BlockSpec((GLA_C, GLA_HV), lambda b, c: (rows(b, c), 0)),
            pl.BlockSpec((GLA_C, GLA_HK), lambda b, c: (rows(b, c), 0)),
            pl.BlockSpec((1, GLA_DV), lambda b, c: (0, 0)),
        ],
        out_specs=[
            pl.BlockSpec((GLA_C, GLA_HV), lambda b, c: (rows(b, c), 0)),
            pl.BlockSpec((1, GLA_H, GLA_DK, GLA_DV), lambda b, c: (b, 0, 0, 0)),
        ],
        out_shape=[
            jax.ShapeDtypeStruct((NP_TOK, GLA_HV), BF16),
            jax.ShapeDtypeStruct((BATCH, GLA_H, GLA_DK, GLA_DV), F32),
        ],
        compiler_params=_params(("arbitrary", "arbitrary")),
        name="gla_prompt",
    )(proj, proj, og, gk, onorm_g.reshape(1, GLA_DV))


def _gla_sample_kernel(qk_ref, v_ref, og_ref, gk_ref, g_ref, s0_ref, *rest, aliased):
    a_ref, s_ref = rest[1:] if aliased else rest
    T = DEC_SEQ
    row = lax.broadcasted_iota(jnp.int32, (T, GLA_DV), 0)
    outs = []
    for h in range(GLA_H):
        ksl = slice(h * GLA_DK, (h + 1) * GLA_DK)
        vsl = slice(h * GLA_DV, (h + 1) * GLA_DV)
        q = qk_ref[:, ksl] * (GLA_DK ** -0.5)
        k = qk_ref[:, GLA_HK + h * GLA_DK:GLA_HK + (h + 1) * GLA_DK]
        v = v_ref[:, vsl]
        gk = gk_ref[:, ksl]
        S = s0_ref[0, 0, h]

        cum = _cumsum_rows(gk)
        last = cum[T - 1:T, :]
        o = _dot((q * jnp.exp(cum)).astype(BF16), S.astype(BF16))
        for d in range(T):
            k_d = k if d == 0 else pltpu.roll(k, d, 0)
            cum_d = cum if d == 0 else pltpu.roll(cum, d, 0)
            v_d = v if d == 0 else pltpu.roll(v, d, 0)
            w = jnp.sum(q * k_d * jnp.exp(jnp.minimum(cum - cum_d, 0.0)), axis=-1, keepdims=True)
            o = o + jnp.where(row >= d, w * v_d, 0.0)

        kd = k * jnp.exp(last - cum)
        stacked = jnp.concatenate(
            [kd, jnp.broadcast_to(jnp.exp(last), (T, GLA_DK)), jnp.zeros((128 - 2 * T, GLA_DK), F32)], axis=0)
        st = stacked.T
        vpad = jnp.concatenate([v, jnp.zeros((128 - T, GLA_DV), F32)], axis=0)
        lane = lax.broadcasted_iota(jnp.int32, (GLA_DK, 128), 1)
        kdt = jnp.where(lane < T, st, 0.0).astype(BF16)
        s_ref[0, 0, h] = st[:, T:T + 1] * S + _dot(kdt, vpad.astype(BF16))

        outs.append(_head_rmsnorm_gate(o, g_ref[...], og_ref[:, vsl]))
    a_ref[...] = jnp.concatenate(outs, axis=1).astype(BF16)


def gla_sample(proj, og, gk, onorm_g, state_all, layer, states_out=None):
    r0 = NP_TOK // DEC_SEQ
    n_a = state_all.shape[0]
    st_block = (1, 1, GLA_H, GLA_DK, GLA_DV)
    in_specs = [
        pl.BlockSpec((DEC_SEQ, 2 * GLA_HK), lambda b: (r0 + b, 0)),
        pl.BlockSpec((DEC_SEQ, GLA_HV), lambda b: (r0 + b, 1)),
        pl.BlockSpec((DEC_SEQ, GLA_HV), lambda b: (r0 + b, 0)),
        pl.BlockSpec((DEC_SEQ, GLA_HK), lambda b: (r0 + b, 0)),
        pl.BlockSpec((1, GLA_DV), lambda b: (0, 0)),
        pl.BlockSpec(st_block, lambda b: (layer, b, 0, 0, 0)),
    ]
    args = [proj, proj, og, gk, onorm_g.reshape(1, GLA_DV), state_all]
    aliases = {}
    if states_out is not None:
        in_specs.append(pl.BlockSpec(memory_space=pl.ANY))
        args.append(states_out)
        aliases = {len(args) - 1: 1}
    return pl.pallas_call(
        functools.partial(_gla_sample_kernel, aliased=states_out is not None),
        grid=(DEC_BATCH,),
        in_specs=in_specs,
        out_specs=[
            pl.BlockSpec((DEC_SEQ, GLA_HV), lambda b: (b, 0)),
            pl.BlockSpec(st_block, lambda b: (layer, b, 0, 0, 0)),
        ],
        out_shape=[
            jax.ShapeDtypeStruct((NS_TOK, GLA_HV), BF16),
            jax.ShapeDtypeStruct((n_a, DEC_BATCH, GLA_H, GLA_DK, GLA_DV), F32),
        ],
        input_output_aliases=aliases,
        compiler_params=_params(("arbitrary",)),
        name="gla_sample",
    )(*args)


def _rope_partner(x):
    lane = lax.broadcasted_iota(jnp.int32, x.shape, 1)
    return jnp.where((lane & (MLA_ROPE - 1)) < MLA_ROPE // 2, pltpu.roll(x, x.shape[1] - MLA_ROPE // 2, 1),
                     pltpu.roll(x, MLA_ROPE // 2, 1))


def _kvprep_kernel(ckv_ref, kr_ref, g_ref, ckvn_ref, krr_ref, c4_ref, s4_ref, *, np_tiles):
    i = pl.program_id(0)
    x = ckv_ref[...]
    ms = jnp.mean(x * x, axis=-1, keepdims=True)
    ckvn_ref[...] = x * lax.rsqrt(ms + EPS) * g_ref[...]

    r = i * TM + lax.broadcasted_iota(jnp.int32, (TM, 128), 0)
    pos = jnp.where(i < np_tiles, r & (SEQ - 1), PAST + (r & (DEC_SEQ - 1))).astype(F32)
    k2 = (lax.broadcasted_iota(jnp.int32, (8, 128), 1) & (MLA_ROPE // 2 - 1)) * 2
    inv = jnp.power(jnp.float32(ROPE_BASE), -k2.astype(F32) / MLA_ROPE)[0:1, :]
    ang = pos * inv
    lane = lax.broadcasted_iota(jnp.int32, (TM, 128), 1)
    c4 = jnp.cos(ang)
    s4 = jnp.where((lane & (MLA_ROPE - 1)) < MLA_ROPE // 2, -jnp.sin(ang), jnp.sin(ang))
    c4_ref[...] = c4
    s4_ref[...] = s4
    kr = kr_ref[...]
    krr_ref[...] = (kr * c4 + _rope_partner(kr) * s4)[:, :MLA_ROPE]


def mla_kvprep(proj, kv_norm_g):
    np_tiles = NP_TOK // TM
    return pl.pallas_call(
        functools.partial(_kvprep_kernel, np_tiles=np_tiles),
        grid=(N_TOK // TM,),
        in_specs=[
            pl.BlockSpec((TM, MLA_RANK), lambda i: (i, 3072 // MLA_RANK)),
            pl.BlockSpec((TM, 128), lambda i: (i, 5632 // 128)),
            pl.BlockSpec((1, MLA_RANK), lambda i: (0, 0)),
        ],
        out_specs=[
            pl.BlockSpec((TM, MLA_RANK), lambda i: (i, 0)),
            pl.BlockSpec((TM, MLA_ROPE), lambda i: (i, 0)),
            pl.BlockSpec((TM, 128), lambda i: (i, 0)),
            pl.BlockSpec((TM, 128), lambda i: (i, 0)),
        ],
        out_shape=[
            jax.ShapeDtypeStruct((N_TOK, MLA_RANK), F32),
            jax.ShapeDtypeStruct((N_TOK, MLA_ROPE), F32),
            jax.ShapeDtypeStruct((N_TOK, 128), F32),
            jax.ShapeDtypeStruct((N_TOK, 128), F32),
        ],
        compiler_params=_params(("arbitrary",)),
        name="mla_kvprep",
    )(proj, proj, kv_norm_g.reshape(1, MLA_RANK))


def _kvup_kernel(ckv_ref, kr_ref, wuk_ref, wuv_ref, k_ref, v_ref):
    c = ckv_ref[...].astype(BF16)
    kn = _dot(c, wuk_ref[...]).astype(BF16)
    vv = _dot(c, wuv_ref[...]).astype(BF16)
    kr = kr_ref[...].astype(BF16)
    for h in range(MLA_H):
        k_ref[h, :, :MLA_NOPE] = kn[:, h * MLA_NOPE:(h + 1) * MLA_NOPE]
        k_ref[h, :, MLA_NOPE:] = kr
        v_ref[h] = vv[:, h * MLA_VH:(h + 1) * MLA_VH]


def mla_kvup(ckv_n, kr_r, w_uk2, w_uv2):
    return pl.pallas_call(
        _kvup_kernel,
        grid=(NP_TOK // TM,),
        in_specs=[
            pl.BlockSpec((TM, MLA_RANK), lambda i: (i, 0)),
            pl.BlockSpec((TM, MLA_ROPE), lambda i: (i, 0)),
            pl.BlockSpec((MLA_RANK, MLA_H * MLA_NOPE), lambda i: (0, 0)),
            pl.BlockSpec((MLA_RANK, MLA_H * MLA_VH), lambda i: (0, 0)),
        ],
        out_specs=[
            pl.BlockSpec((MLA_H, TM, MLA_QK), lambda i: (0, i, 0)),
            pl.BlockSpec((MLA_H, TM, MLA_VH), lambda i: (0, i, 0)),
        ],
        out_shape=[
            jax.ShapeDtypeStruct((MLA_H, NP_TOK, MLA_QK), BF16),
            jax.ShapeDtypeStruct((MLA_H, NP_TOK, MLA_VH), BF16),
        ],
        compiler_params=_params(("arbitrary",)),
        name="mla_kvup",
    )(ckv_n, kr_r, w_uk2, w_uv2)


FLASH_T = 512


def _rope_pair_select(x2, h):
    return jnp.where(h % 2 == 1, pltpu.roll(x2, MLA_ROPE, 1), x2)[:, :MLA_ROPE]


FLASH_G = 4


def _lanes(x, n):
    return x if n == 128 else jnp.concatenate([x] * (n // 128), axis=1)


def _flash_kernel(qt_ref, kt_ref, qn_ref, qr_ref, c4_ref, s4_ref, k_ref, v_ref, gate_ref, o_ref,
                  q_sc, m_sc, l_sc, acc_sc):
    s_id = pl.program_id(2)
    qi = qt_ref[s_id]
    ki = kt_ref[s_id]
    T, G = FLASH_T, FLASH_G

    @pl.when(ki == 0)
    def _():
        x = qr_ref[...]
        c = _lanes(c4_ref[...], G * MLA_ROPE)
        s = _lanes(s4_ref[...], G * MLA_ROPE)
        rot = (x * c + _rope_partner(x) * s) * MLA_SCALE
        for g in range(G):
            q_sc[g, :, :MLA_NOPE] = (qn_ref[:, g * MLA_NOPE:(g + 1) * MLA_NOPE] * MLA_SCALE).astype(BF16)
            q_sc[g, :, MLA_NOPE:] = rot[:, g * MLA_ROPE:(g + 1) * MLA_ROPE].astype(BF16)
        m_sc[...] = jnp.full_like(m_sc, _NEG)
        l_sc[...] = jnp.zeros_like(l_sc)
        acc_sc[...] = jnp.zeros_like(acc_sc)

    def step(masked):
        for g in range(G):
            s = _dot_nt(q_sc[g], k_ref[g])
            if masked:
                r = lax.broadcasted_iota(jnp.int32, (T, T), 0)
                c = lax.broadcasted_iota(jnp.int32, (T, T), 1)
                s = jnp.where(c <= r, s, _NEG)
            m_prev = m_sc[g]
            m_new = jnp.maximum(m_prev, jnp.max(s, axis=-1, keepdims=True))
            alpha = jnp.exp(m_prev - m_new)
            p = jnp.exp(s - _lanes(m_new, T))
            l_sc[g] = alpha * l_sc[g] + jnp.sum(p, axis=-1, keepdims=True)
            acc_sc[g] = alpha * acc_sc[g] + _dot(p.astype(BF16), v_ref[g])
            m_sc[g] = m_new

    @pl.when(ki < qi)
    def _():
        step(False)

    @pl.when(ki == qi)
    def _():
        step(True)
        for g in range(G):
            sl = slice(g * MLA_VH, (g + 1) * MLA_VH)
            o = acc_sc[g] / l_sc[g]
            o_ref[:, sl] = (o * _silu(gate_ref[:, sl])).astype(BF16)


def mla_flash(proj, c4, s4, kcat, vv):
    T, G = FLASH_T, FLASH_G
    nq = SEQ // T
    pairs = [(qi, ki) for qi in range(nq) for ki in range(qi + 1)]
    qt = jnp.asarray(np.array([p[0] for p in pairs], np.int32))
    kt = jnp.asarray(np.array([p[1] for p in pairs], np.int32))

    def qrow(b, g, s, qt, kt):
        return b * nq + qt[s]

    def krow(b, g, s, qt, kt):
        return b * nq + kt[s]

    grid_spec = pltpu.PrefetchScalarGridSpec(
        num_scalar_prefetch=2,
        grid=(BATCH, MLA_H // G, len(pairs)),
        in_specs=[
            pl.BlockSpec((T, G * MLA_NOPE), lambda b, g, s, qt, kt: (qrow(b, g, s, qt, kt), g)),
            pl.BlockSpec((T, G * MLA_ROPE),
                         lambda b, g, s, qt, kt: (qrow(b, g, s, qt, kt), 2048 // (G * MLA_ROPE) + g)),
            pl.BlockSpec((T, 128), lambda b, g, s, qt, kt: (qrow(b, g, s, qt, kt), 0)),
            pl.BlockSpec((T, 128), lambda b, g, s, qt, kt: (qrow(b, g, s, qt, kt), 0)),
            pl.BlockSpec((G, T, MLA_QK), lambda b, g, s, qt, kt: (g, krow(b, g, s, qt, kt), 0)),
            pl.BlockSpec((G, T, MLA_VH), lambda b, g, s, qt, kt: (g, krow(b, g, s, qt, kt), 0)),
            pl.BlockSpec((T, G * MLA_VH),
                         lambda b, g, s, qt, kt: (qrow(b, g, s, qt, kt), 3584 // (G * MLA_VH) + g)),
        ],
        out_specs=pl.BlockSpec((T, G * MLA_VH), lambda b, g, s, qt, kt: (qrow(b, g, s, qt, kt), g)),
        scratch_shapes=[
            pltpu.VMEM((G, T, MLA_QK), BF16),
            pltpu.VMEM((G, T, 128), F32),
            pltpu.VMEM((G, T, 128), F32),
            pltpu.VMEM((G, T, MLA_VH), F32),
        ],
    )
    return pl.pallas_call(
        _flash_kernel,
        grid_spec=grid_spec,
        out_shape=jax.ShapeDtypeStruct((NP_TOK, MLA_H * MLA_VH), BF16),
        compiler_params=_params(("arbitrary", "arbitrary", "arbitrary")),
        name="mla_flash",
    )(qt, kt, proj, proj, c4, s4, kcat, vv, proj)


def _qabs_kernel(qn_ref, qr_ref, c4_ref, s4_ref, wukt_ref, o_ref):
    h = pl.program_id(0)
    ql = _dot(qn_ref[...].astype(BF16), wukt_ref[0]) * MLA_SCALE
    x2 = qr_ref[...]
    rot = x2 * c4_ref[...] + _rope_partner(x2) * s4_ref[...]
    qr = _rope_pair_select(rot, h) * MLA_SCALE
    o_ref[:, 0, :, :MLA_RANK] = ql.reshape(DEC_BATCH, DEC_SEQ, MLA_RANK)
    o_ref[:, 0, :, MLA_RANK:] = qr.reshape(DEC_BATCH, DEC_SEQ, MLA_ROPE)


def mla_qabs(proj, c4, s4, w_ukt3):
    rb = NP_TOK // NS_TOK
    return pl.pallas_call(
        _qabs_kernel,
        grid=(MLA_H,),
        in_specs=[
            pl.BlockSpec((NS_TOK, 128), lambda h: (rb, h)),
            pl.BlockSpec((NS_TOK, 128), lambda h: (rb, 2048 // 128 + h // 2)),
            pl.BlockSpec((NS_TOK, 128), lambda h: (rb, 0)),
            pl.BlockSpec((NS_TOK, 128), lambda h: (rb, 0)),
            pl.BlockSpec((1, MLA_NOPE, MLA_RANK), lambda h: (h, 0, 0)),
        ],
        out_specs=pl.BlockSpec((DEC_BATCH, 1, DEC_SEQ, MLA_LAT), lambda h: (0, h, 0, 0)),
        out_shape=jax.ShapeDtypeStruct((DEC_BATCH, MLA_H, DEC_SEQ, MLA_LAT), F32),
        compiler_params=_params(("arbitrary",)),
        name="mla_qabs",
    )(proj, proj, c4, s4, w_ukt3)


DEC_SEQS = 2
DEC_PG = 32
DEC_NG = N_PAGES // DEC_PG
DEC_SLOTS = 2
DEC_AHEAD = DEC_SLOTS - 1


def _decode_kernel(pt_ref, q_ref, cnew_ref, knew_ref, ckv_hbm, krt_hbm, o_ref,
                   kv_buf, kr_buf, sem, m_sc, l_sc, acc_sc, *, layer):
    step = pl.program_id(0)
    n_groups = pl.num_programs(0) * DEC_NG
    R = MLA_H * DEC_SEQ

    def group_copies(n, slot):
        first_page = lax.div(n, DEC_NG) * (DEC_SEQS * N_PAGES) + lax.rem(n, DEC_NG) * DEC_PG
        cps = []
        for s in range(DEC_SEQS):
            for r in range(DEC_PG):
                page = pt_ref[first_page + s * N_PAGES + r]
                j = s * DEC_PG + r
                cps.append(pltpu.make_async_copy(ckv_hbm.at[layer, page], kv_buf.at[slot, j], sem.at[0, slot]))
                cps.append(pltpu.make_async_copy(krt_hbm.at[layer, page], kr_buf.at[slot, j], sem.at[1, slot]))
        return cps

    def start_group(n, slot):
        for i, cp in enumerate(group_copies(n, slot)):
            cp.start(priority=(i // 2) % 2)

    @pl.when(step == 0)
    def _():
        for n0 in range(DEC_AHEAD):
            start_group(n0, n0)

    m_sc[...] = jnp.full_like(m_sc, _NEG)
    l_sc[...] = jnp.zeros_like(l_sc)
    acc_sc[...] = jnp.zeros_like(acc_sc)

    ql, qr = [], []
    for s in range(DEC_SEQS):
        q = q_ref[s].reshape(R, MLA_LAT)
        ql.append(q[:, :MLA_RANK].astype(BF16))
        qr.append(q[:, MLA_RANK:].astype(BF16))

    def online(s, scores, vals):
        m_prev = m_sc[s]
        m_new = jnp.maximum(m_prev, jnp.max(scores, axis=-1, keepdims=True))
        alpha = jnp.exp(m_prev - m_new)
        p = jnp.exp(scores - _lanes(m_new, scores.shape[1]))
        l_sc[s] = alpha * l_sc[s] + jnp.sum(p, axis=-1, keepdims=True)
        acc_sc[s] = _lanes(alpha, MLA_RANK) * acc_sc[s] + _dot(p.astype(BF16), vals)
        m_sc[s] = m_new

    def group(g, carry):
        n = step * DEC_NG + g
        slot = lax.rem(n, DEC_SLOTS)
        for cp in group_copies(n, slot):
            cp.wait()

        @pl.when(n + DEC_AHEAD < n_groups)
        def _():
            start_group(n + DEC_AHEAD, lax.rem(n + DEC_AHEAD, DEC_SLOTS))

        for s in range(DEC_SEQS):
            kv = jnp.concatenate(
                [kv_buf[slot, s * DEC_PG + r].astype(BF16) for r in range(DEC_PG)], axis=0)
            krt = jnp.concatenate(
                [kr_buf[slot, s * DEC_PG + r].astype(BF16) for r in range(DEC_PG)], axis=1)
            online(s, _dot_nt(ql[s], kv) + _dot(qr[s], krt), kv)
        return carry

    lax.fori_loop(0, DEC_NG, group, 0)

    t = lax.broadcasted_iota(jnp.int32, (R, PAGE), 0) & (DEC_SEQ - 1)
    j = lax.broadcasted_iota(jnp.int32, (R, PAGE), 1)
    for s in range(DEC_SEQS):
        rows = slice(s * DEC_SEQ, (s + 1) * DEC_SEQ)
        cn = jnp.concatenate([cnew_ref[rows, :], jnp.zeros((PAGE - DEC_SEQ, MLA_RANK), F32)], axis=0).astype(BF16)
        kn = jnp.concatenate([knew_ref[rows, :], jnp.zeros((PAGE - DEC_SEQ, MLA_ROPE), F32)], axis=0).astype(BF16)
        sc = _dot_nt(ql[s], cn) + _dot_nt(qr[s], kn)
        online(s, jnp.where(j <= t, sc, _NEG), cn)
        o = acc_sc[s] / _lanes(l_sc[s], MLA_RANK)
        o_ref[s] = o.reshape(MLA_H, DEC_SEQ, MLA_RANK)


def mla_decode(qcat, cache_ckv, cache_krt, layer, page_table, ckv_n, kr_r):
    rows = DEC_SEQS * DEC_SEQ
    r0 = NP_TOK // rows
    n_pg = DEC_SEQS * DEC_PG
    R = MLA_H * DEC_SEQ
    grid_spec = pltpu.PrefetchScalarGridSpec(
        num_scalar_prefetch=1,
        grid=(DEC_BATCH // DEC_SEQS,),
        in_specs=[
            pl.BlockSpec((DEC_SEQS, MLA_H, DEC_SEQ, MLA_LAT), lambda i, pt: (i, 0, 0, 0)),
            pl.BlockSpec((rows, MLA_RANK), lambda i, pt: (r0 + i, 0)),
            pl.BlockSpec((rows, MLA_ROPE), lambda i, pt: (r0 + i, 0)),
            pl.BlockSpec(memory_space=pl.ANY),
            pl.BlockSpec(memory_space=pl.ANY),
        ],
        out_specs=pl.BlockSpec((DEC_SEQS, MLA_H, DEC_SEQ, MLA_RANK), lambda i, pt: (i, 0, 0, 0)),
        scratch_shapes=[
            pltpu.VMEM((DEC_SLOTS, n_pg, PAGE, MLA_RANK), F32),
            pltpu.VMEM((DEC_SLOTS, n_pg, MLA_ROPE, PAGE), F32),
            pltpu.SemaphoreType.DMA((2, DEC_SLOTS)),
            pltpu.VMEM((DEC_SEQS, R, 128), F32),
            pltpu.VMEM((DEC_SEQS, R, 128), F32),
            pltpu.VMEM((DEC_SEQS, R, MLA_RANK), F32),
        ],
    )
    return pl.pallas_call(
        functools.partial(_decode_kernel, layer=layer),
        grid_spec=grid_spec,
        out_shape=jax.ShapeDtypeStruct((DEC_BATCH, MLA_H, DEC_SEQ, MLA_RANK), F32),
        compiler_params=_params(("arbitrary",)),
        name="mla_decode",
    )(page_table.reshape(-1), qcat, ckv_n, kr_r, cache_ckv, cache_krt)


def _uvup_kernel(ol_ref, wuv_ref, gate_ref, a_ref):
    ol = ol_ref[...].reshape(NS_TOK, MLA_RANK).astype(BF16)
    a_ref[...] = (_dot(ol, wuv_ref[0]) * _silu(gate_ref[...])).astype(BF16)


def mla_uvup(o_lat, w_uv3, proj):
    rb = NP_TOK // NS_TOK
    return pl.pallas_call(
        _uvup_kernel,
        grid=(MLA_H,),
        in_specs=[
            pl.BlockSpec((DEC_BATCH, 1, DEC_SEQ, MLA_RANK), lambda h: (0, h, 0, 0)),
            pl.BlockSpec((1, MLA_RANK, MLA_VH), lambda h: (h, 0, 0)),
            pl.BlockSpec((NS_TOK, 128), lambda h: (rb, 3584 // 128 + h)),
        ],
        out_specs=pl.BlockSpec((NS_TOK, MLA_VH), lambda h: (0, h)),
        out_shape=jax.ShapeDtypeStruct((NS_TOK, MLA_H * MLA_VH), BF16),
        compiler_params=_params(("arbitrary",)),
        name="mla_uvup",
    )(o_lat, w_uv3, proj)


POOL_HALO = 16


def _pool_prompt_kernel(u_ref, halo_ref, gate_ref, wg_ref, sc_ref, a_ref, tail_ref, ext_sc, *, tiles_per_seq):
    i = pl.program_id(0)
    first = (i % tiles_per_seq) == 0
    tail_ref[0] = u_ref[TM - POOL_HALO:, :]
    ext_sc[:POOL_HALO, :] = jnp.where(first, 0.0, halo_ref[...])
    ext_sc[POOL_HALO:, :] = u_ref[...]
    t = (i % tiles_per_seq) * TM + lax.broadcasted_iota(jnp.int32, (TM, 1), 0)

    for gi, w in enumerate(POOL_WINDOWS):
        cols = slice(gi * POOL_G, (gi + 1) * POOL_G)
        u = u_ref[:, cols]
        acc = u
        for j in range(1, w):
            acc = acc + ext_sc[POOL_HALO - j:POOL_HALO - j + TM, cols]
        cnt = jnp.minimum(t + 1, w).astype(F32)
        p = (acc / cnt - u).astype(BF16)
        z = _dot(p, wg_ref[gi]) * sc_ref[:, cols]
        a_ref[:, cols] = (z * _silu(gate_ref[:, cols])).astype(BF16)


def pool_prompt(proj, w_grp, pscale):
    tiles_per_seq = SEQ // TM
    hb = TM // POOL_HALO
    ng = len(POOL_WINDOWS)
    return pl.pallas_call(
        functools.partial(_pool_prompt_kernel, tiles_per_seq=tiles_per_seq),
        grid=(NP_TOK // TM,),
        in_specs=[
            pl.BlockSpec((TM, D), lambda i: (i, 0)),
            pl.BlockSpec((POOL_HALO, D), lambda i: (jnp.maximum(i * hb - 1, 0), 0)),
            pl.BlockSpec((TM, D), lambda i: (i, 1)),
            pl.BlockSpec((ng, POOL_G, POOL_G), lambda i: (0, 0, 0)),
            pl.BlockSpec((1, D), lambda i: (0, 0)),
        ],
        out_specs=[
            pl.BlockSpec((TM, D), lambda i: (i, 0)),
            pl.BlockSpec((1, POOL_HALO, D), lambda i: (i, 0, 0)),
        ],
        out_shape=[
            jax.ShapeDtypeStruct((NP_TOK, D), BF16),
            jax.ShapeDtypeStruct((NP_TOK // TM, POOL_HALO, D), F32),
        ],
        scratch_shapes=[pltpu.VMEM((POOL_HALO + TM, D), F32)],
        compiler_params=_params(("arbitrary",)),
        name="pool_prompt",
    )(proj, proj, proj, w_grp, pscale)


def _pool_sample_kernel(u_ref, gate_ref, hist_ref, wg_ref, sc_ref, a_ref):
    g = pl.program_id(0)

    def seq(r):
        return hist_ref[r] if r < POOL_HIST else u_ref[r - POOL_HIST]

    for gi, w in enumerate(POOL_WINDOWS):
        @pl.when(g == gi)
        def _(w=w):
            ps = []
            for t in range(DEC_SEQ):
                acc = seq(POOL_HIST + t)
                for j in range(1, w):
                    acc = acc + seq(POOL_HIST + t - j)
                ps.append(acc / float(w) - u_ref[t])
            p = jnp.concatenate(ps, axis=0).astype(BF16)
            z = _dot(p, wg_ref[0]) * sc_ref[...]
            gate = gate_ref[...].reshape(NS_TOK, POOL_G)
            a_ref[...] = (z * _silu(gate)).astype(BF16).reshape(DEC_SEQ, DEC_BATCH, POOL_G)


def pool_sample(proj_t, hist_t, w_grp, pscale):
    ng = len(POOL_WINDOWS)
    return pl.pallas_call(
        _pool_sample_kernel,
        grid=(ng,),
        in_specs=[
            pl.BlockSpec((DEC_SEQ, DEC_BATCH, POOL_G), lambda g: (0, 0, g)),
            pl.BlockSpec((DEC_SEQ, DEC_BATCH, POOL_G), lambda g: (0, 0, ng + g)),
            pl.BlockSpec((POOL_HIST, DEC_BATCH, POOL_G), lambda g: (0, 0, g)),
            pl.BlockSpec((1, POOL_G, POOL_G), lambda g: (g, 0, 0)),
            pl.BlockSpec((1, POOL_G), lambda g: (0, g)),
        ],
        out_specs=pl.BlockSpec((DEC_SEQ, DEC_BATCH, POOL_G), lambda g: (0, 0, g)),
        out_shape=jax.ShapeDtypeStruct((DEC_SEQ, DEC_BATCH, D), BF16),
        compiler_params=_params(("arbitrary",)),
        name="pool_sample",
    )(proj_t, proj_t, hist_t, w_grp, pscale)


def kernel(x_prompt, x_sample, c_prompt, c_sample, state_gla, cache_ckv, cache_kr, state_pool, page_table, norm_g, ada_w, ada_b, final_norm_g, gla_w_in, gla_w_gate_up, gla_b_gate, gla_onorm_g, gla_w_out, mla_w_in, mla_kv_norm_g, mla_w_uk, mla_w_uv, mla_w_out, pool_w_in, pool_w_grp, pool_scale, pool_w_out):
    x_all = (x_prompt.reshape(NP_TOK, D), x_sample.reshape(NS_TOK, D))

    n_c = BATCH + DEC_BATCH
    c_all = jnp.concatenate([c_prompt, c_sample, jnp.zeros((8 - n_c % 8, D), F32)], axis=0)
    mod = ada_mod(c_all, ada_w, ada_b)

    gla_states_p, gla_states_s = [], None
    ckv_rows, kr_rows, pool_p, pool_s = [], [], [], []
    ia = ib = ic = 0
    mods = [(mod[l, :BATCH].reshape(BATCH, 1, 3 * D), jnp.repeat(mod[l, BATCH:n_c], DEC_SEQ, axis=0))
            for l in range(DEPTH)]
    h = norm_mod(x_all, norm_g[0], *mods[0])
    for l in range(DEPTH):
        mod_p, mod_s = mods[l]
        mixer = LAYER_MIXER[l]
        if mixer == 0:
            w_t = jnp.swapaxes(gla_w_in[ia], 0, 1).astype(BF16)
            cut = 2 * GLA_HK + GLA_HV
            w_glr = jnp.pad(w_t[cut:cut + GLA_RANK], ((0, 128 - GLA_RANK), (0, 0)))
            w_up = jnp.pad(gla_w_gate_up[ia], ((0, 128 - GLA_RANK), (0, 0))).astype(BF16)
            proj = proj_matmul(h, w_t, 1024, True, n=cut)
            og = proj_matmul(h, w_t[cut + GLA_RANK:], 1024, True)
            gk = gla_gate(h, w_glr, w_up, gla_b_gate[ia].reshape(1, GLA_HK))
            a_p, st_p = gla_prompt(proj, og, gk, gla_onorm_g[ia])
            a_s, gla_states_s = gla_sample(proj, og, gk, gla_onorm_g[ia], state_gla, ia, states_out=gla_states_s)
            gla_states_p.append(st_p)
            w_out = gla_w_out[ia].astype(BF16)
            ia += 1
        elif mixer == 1:
            w_t = jnp.swapaxes(mla_w_in[ib], 0, 1)
            nq = MLA_H * MLA_QK
            wq = w_t[:nq].reshape(MLA_H, MLA_QK, D)
            w_perm = jnp.concatenate([
                wq[:, :MLA_NOPE].reshape(MLA_H * MLA_NOPE, D),
                wq[:, MLA_NOPE:].reshape(MLA_H * MLA_ROPE, D),
                w_t[nq:nq + MLA_RANK],
                w_t[nq + MLA_RANK + MLA_ROPE:],
                w_t[nq + MLA_RANK:nq + MLA_RANK + MLA_ROPE],
                jnp.zeros((128 - MLA_ROPE, D), F32),
            ], axis=0).astype(BF16)
            proj = proj_matmul(h, w_perm, 1152, True)
            ckv_n, kr_r, c4, s4 = mla_kvprep(proj, mla_kv_norm_g[ib])
            w_uk2 = mla_w_uk[ib].reshape(MLA_RANK, MLA_H * MLA_NOPE).astype(BF16)
            w_uv2 = mla_w_uv[ib].reshape(MLA_RANK, MLA_H * MLA_VH).astype(BF16)
            w_ukt3 = jnp.transpose(mla_w_uk[ib], (1, 2, 0)).astype(BF16)
            w_uv3 = jnp.transpose(mla_w_uv[ib], (1, 0, 2)).astype(BF16)
            kcat, vv = mla_kvup(ckv_n, kr_r, w_uk2, w_uv2)
            a_p = mla_flash(proj, c4, s4, kcat, vv)
            qcat = mla_qabs(proj, c4, s4, w_ukt3)
            cache_krt = jnp.swapaxes(cache_kr, 2, 3)
            o_lat = mla_decode(qcat, cache_ckv, cache_krt, ib, page_table, ckv_n, kr_r)
            a_s = mla_uvup(o_lat, w_uv3, proj)
            ckv_rows.append(ckv_n)
            kr_rows.append(kr_r)
            w_out = mla_w_out[ib].astype(BF16)
            ib += 1
        else:
            proj = proj_matmul(h, pool_w_in[ic].astype(BF16), 1024, False)
            w_grp = pool_w_grp[ic].astype(BF16)
            pscale = pool_scale[ic].reshape(1, D)
            a_p, u_tails = pool_prompt(proj, w_grp, pscale)
            proj_t = jnp.transpose(proj[NP_TOK:].reshape(DEC_BATCH, DEC_SEQ, 2 * D), (1, 0, 2))
            hist_t = jnp.transpose(state_pool[ic], (1, 0, 2))
            a_st = pool_sample(proj_t, hist_t, w_grp, pscale)
            a_s = jnp.transpose(a_st, (1, 0, 2)).reshape(NS_TOK, D)
            u_s = proj[NP_TOK:, :D].reshape(DEC_BATCH, DEC_SEQ, D)
            pool_p.append(u_tails.reshape(BATCH, SEQ // TM, POOL_HALO, D)[:, -1, POOL_HALO - POOL_HIST:, :])
            pool_s.append(jnp.concatenate([state_pool[ic][:, DEC_SEQ:, :], u_s], axis=1))
            w_out = pool_w_out[ic].astype(BF16)
            ic += 1
        if l == DEPTH - 1:
            y_p, y_s = out_proj(a_p, a_s, w_out, x_all, mod_p, mod_s, final_g=final_norm_g)
        else:
            x_all, h = out_proj(a_p, a_s, w_out, x_all, mod_p, mod_s, next_norm=(norm_g[l + 1],) + mods[l + 1])

    y_prompt = y_p.reshape(BATCH, SEQ, D)
    y_sample = y_s.reshape(DEC_BATCH, DEC_SEQ, D)
    ckv_all = jnp.stack(ckv_rows)
    kr_all = jnp.stack(kr_rows)
    return (
        y_prompt,
        y_sample,
        jnp.stack(gla_states_p),
        gla_states_s,
        ckv_all[:, :NP_TOK].reshape(-1, BATCH, SEQ, MLA_RANK),
        kr_all[:, :NP_TOK].reshape(-1, BATCH, SEQ, MLA_ROPE),
        ckv_all[:, NP_TOK:].reshape(-1, DEC_BATCH, DEC_SEQ, MLA_RANK),
        kr_all[:, NP_TOK:].reshape(-1, DEC_BATCH, DEC_SEQ, MLA_ROPE),
        jnp.stack(pool_p),
        jnp.stack(pool_s),
    )
```
